```python
import jax, jax.numpy as jnp
from jax import lax
import numpy as np

D_MODEL = 2048
BATCH = 1
SEQ = 8192
DEPTH = 2

CHUNK = 64
A_WIDTH = 1024
A_GROUPS = 4
A_HEAD = A_WIDTH // A_GROUPS
A_BLOCK = 128
B_WIDTH = 1024
CONV_W = 3
C_WIDTH = 1024
POOL_WINDOWS = (2, 4, 8, 16)
C_GROUPS = len(POOL_WINDOWS)
C_HEAD = C_WIDTH // C_GROUPS
N_BRANCH = 3
IN_COLS = 2 * A_WIDTH + 3 * B_WIDTH + C_WIDTH
N_EXPERTS = 64
TOP_K = 8
N_GROUPS = 8
TOPK_GROUPS = 4
EXPERT_HIDDEN = 512
SHARED_HIDDEN = 512
ROUTED_SCALE = 2.5
ALPHA = (2.0 * DEPTH) ** 0.25
BETA = (8.0 * DEPTH) ** -0.25
LN_EPS = 1e-5

kernel_name = "hybrid_gmlp_conv_pool_moe_deepnorm"


def layer_norm(x, g, b):
    xf = x.astype(jnp.float32)
    mu = jnp.mean(xf, axis=-1, keepdims=True)
    var = jnp.mean(jnp.square(xf - mu), axis=-1, keepdims=True)
    y = (xf - mu) * lax.rsqrt(var + LN_EPS)
    return (y * g.astype(jnp.float32) + b.astype(jnp.float32)).astype(x.dtype)


def normalize(x):
    xf = x.astype(jnp.float32)
    mu = jnp.mean(xf, axis=-1, keepdims=True)
    var = jnp.mean(jnp.square(xf - mu), axis=-1, keepdims=True)
    return ((xf - mu) * lax.rsqrt(var + LN_EPS)).astype(x.dtype)


def spatial_gating(u, v, w_s, b_s):
    bsz, s, _ = u.shape
    v = normalize(v.reshape(bsz, s // A_BLOCK, A_BLOCK, A_GROUPS, A_HEAD))
    chunk_id = jnp.arange(A_BLOCK) // CHUNK
    mask = chunk_id[None, :] <= chunk_id[:, None]
    w = jnp.where(mask[None], w_s, 0)
    mixed = jnp.einsum('gij,bnjgc->bnigc', w, v) + b_s.T[None, None, :, :, None]
    return u * mixed.reshape(bsz, s, A_WIDTH)


def short_conv(b_gate, c_gate, h, w_conv):
    z = c_gate * h
    s = z.shape[1]
    zp = jnp.pad(z, ((0, 0), (CONV_W - 1, 0), (0, 0)))
    conv = w_conv[0] * zp[:, 0:s]
    for k in range(1, CONV_W):
        conv = conv + w_conv[k] * zp[:, k:k + s]
    return b_gate * conv


def multiscale_pool(p, w_pool, pool_scale):
    bsz, s, _ = p.shape
    pf = p.astype(jnp.float32).reshape(bsz, s, C_GROUPS, C_HEAD)
    cs = jnp.pad(jnp.cumsum(pf, axis=1), ((0, 0), (1, 0), (0, 0), (0, 0)))
    t = jnp.arange(s)
    means = []
    for g, win in enumerate(POOL_WINDOWS):
        start = jnp.maximum(t + 1 - win, 0)
        count = (t + 1 - start).astype(jnp.float32)
        total = cs[:, 1:, g] - cs[:, start, g]
        means.append(total / count[None, :, None])
    pooled = (jnp.stack(means, axis=2) - pf).astype(p.dtype)
    mixed = jnp.einsum('bsgc,gcd->bsgd', pooled, w_pool).reshape(bsz, s, C_WIDTH)
    return mixed * pool_scale


def mixer_block(x, w_in, w_s, b_s, w_conv, w_pool, pool_scale, w_gate, b_gate,
                w_proj_a, w_proj_b, w_proj_c, w_out):
    bsz, s, d = x.shape
    proj = jnp.einsum('bsd,de->bse', x, w_in)
    splits = [A_WIDTH, 2 * A_WIDTH, 2 * A_WIDTH + B_WIDTH,
              2 * A_WIDTH + 2 * B_WIDTH, 2 * A_WIDTH + 3 * B_WIDTH]
    u, v, bg, cg, h, p = jnp.split(proj, splits, axis=-1)
    y_a = spatial_gating(jax.nn.gelu(u, approximate=False),
                         jax.nn.gelu(v, approximate=False), w_s, b_s)
    y_b = short_conv(bg, cg, h, w_conv)
    y_c = multiscale_pool(p, w_pool, pool_scale)
    gates = jax.nn.sigmoid(jnp.einsum('bsd,de->bse', x, w_gate) + b_gate)
    gates = gates.reshape(bsz, s, N_BRANCH, d)
    merged = (gates[:, :, 0] * jnp.einsum('bsc,cd->bsd', y_a, w_proj_a)
              + gates[:, :, 1] * jnp.einsum('bsc,cd->bsd', y_b, w_proj_b)
              + gates[:, :, 2] * jnp.einsum('bsc,cd->bsd', y_c, w_proj_c))
    return jnp.einsum('bsd,de->bse', merged, w_out)


def swiglu(x, w_g, w_u, w_d):
    return jnp.einsum('th,hd->td', jax.nn.silu(x @ w_g) * (x @ w_u), w_d)


def moe_block(x, w_router, router_bias, w1, w3, w2, ws1, ws3, ws2):
    bsz, s, d = x.shape
    xt = x.reshape(bsz * s, d)
    scores = jax.nn.sigmoid(jnp.einsum('td,de->te', xt, w_router).astype(jnp.float32))
    sel = scores + router_bias.astype(jnp.float32)
    grp = sel.reshape(-1, N_GROUPS, N_EXPERTS // N_GROUPS)
    grp_score = lax.top_k(grp, 2)[0].sum(-1)
    _, grp_idx = lax.top_k(grp_score, TOPK_GROUPS)
    grp_mask = jax.nn.one_hot(grp_idx, N_GROUPS, dtype=jnp.float32).sum(1)
    exp_mask = jnp.repeat(grp_mask, N_EXPERTS // N_GROUPS, axis=-1)
    sel = jnp.where(exp_mask > 0, sel, -jnp.inf)
    _, idx = lax.top_k(sel, TOP_K)
    w = jnp.take_along_axis(scores, idx, axis=-1)
    w = w / jnp.sum(w, axis=-1, keepdims=True) * ROUTED_SCALE
    combine = jnp.sum(jax.nn.one_hot(idx, N_EXPERTS, dtype=jnp.float32) * w[..., None], axis=1)
    combine = combine.astype(x.dtype)
    out = swiglu(xt, ws1, ws3, ws2)
    for e in range(N_EXPERTS):
        out = out + combine[:, e:e + 1] * swiglu(xt, w1[e], w3[e], w2[e])
    return out.reshape(bsz, s, d)


def setup_inputs(seed: int = 0) -> dict:
    key = jax.random.key(seed)
    ks = jax.random.split(key, 32)
    f32 = jnp.float32

    def nrm(k, shape, scale):
        return jax.random.normal(k, shape, f32) * scale

    L = DEPTH
    return {
        "x": nrm(ks[0], (BATCH, SEQ, D_MODEL), 1.0),
        "in_ln_g": 1.0 + nrm(ks[1], (D_MODEL,), 0.02),
        "in_ln_b": nrm(ks[2], (D_MODEL,), 0.02),
        "w_in": nrm(ks[3], (L, D_MODEL, IN_COLS), D_MODEL ** -0.5),
        "w_spatial": nrm(ks[4], (L, A_GROUPS, A_BLOCK, A_BLOCK), A_BLOCK ** -0.5),
        "b_spatial": 1.0 + nrm(ks[5], (L, A_GROUPS, A_BLOCK), 0.02),
        "w_conv": nrm(ks[6], (L, CONV_W, B_WIDTH), CONV_W ** -0.5),
        "w_pool": nrm(ks[7], (L, C_GROUPS, C_HEAD, C_HEAD), C_HEAD ** -0.5),
        "pool_scale": 1.0 + nrm(ks[8], (L, C_WIDTH), 0.02),
        "w_gate": nrm(ks[9], (L, D_MODEL, N_BRANCH * D_MODEL), D_MODEL ** -0.5),
        "b_gate": nrm(ks[10], (L, N_BRANCH * D_MODEL), 0.02),
        "w_proj_a": nrm(ks[11], (L, A_WIDTH, D_MODEL), A_WIDTH ** -0.5),
        "w_proj_b": nrm(ks[12], (L, B_WIDTH, D_MODEL), B_WIDTH ** -0.5),
        "w_proj_c": nrm(ks[13], (L, C_WIDTH, D_MODEL), C_WIDTH ** -0.5),
        "w_out": nrm(ks[14], (L, D_MODEL, D_MODEL), BETA * D_MODEL ** -0.5),
        "ln1_g": 1.0 + nrm(ks[15], (L, D_MODEL), 0.02),
        "ln1_b": nrm(ks[16], (L, D_MODEL), 0.02),
        "w_router": nrm(ks[17], (L, D_MODEL, N_EXPERTS), D_MODEL ** -0.5),
        "router_bias": nrm(ks[18], (L, N_EXPERTS), 0.01),
        "w_expert_gate": nrm(ks[19], (L, N_EXPERTS, D_MODEL, EXPERT_HIDDEN), D_MODEL ** -0.5),
        "w_expert_up": nrm(ks[20], (L, N_EXPERTS, D_MODEL, EXPERT_HIDDEN), D_MODEL ** -0.5),
        "w_expert_down": nrm(ks[21], (L, N_EXPERTS, EXPERT_HIDDEN, D_MODEL), BETA * EXPERT_HIDDEN ** -0.5),
        "w_shared_gate": nrm(ks[22], (L, D_MODEL, SHARED_HIDDEN), D_MODEL ** -0.5),
        "w_shared_up": nrm(ks[23], (L, D_MODEL, SHARED_HIDDEN), D_MODEL ** -0.5),
        "w_shared_down": nrm(ks[24], (L, SHARED_HIDDEN, D_MODEL), BETA * SHARED_HIDDEN ** -0.5),
        "ln2_g": 1.0 + nrm(ks[25], (L, D_MODEL), 0.02),
        "ln2_b": nrm(ks[26], (L, D_MODEL), 0.02),
    }


def reference(x, in_ln_g, in_ln_b, w_in, w_spatial, b_spatial, w_conv, w_pool, pool_scale,
              w_gate, b_gate, w_proj_a, w_proj_b, w_proj_c, w_out, ln1_g, ln1_b,
              w_router, router_bias, w_expert_gate, w_expert_up, w_expert_down,
              w_shared_gate, w_shared_up, w_shared_down, ln2_g, ln2_b):
    x = layer_norm(x, in_ln_g, in_ln_b)
    for l in range(DEPTH):
        mix = mixer_block(x, w_in[l], w_spatial[l], b_spatial[l], w_conv[l], w_pool[l],
                          pool_scale[l], w_gate[l], b_gate[l], w_proj_a[l], w_proj_b[l],
                          w_proj_c[l], w_out[l])
        x = layer_norm(ALPHA * x + mix, ln1_g[l], ln1_b[l])
        ffn = moe_block(x, w_router[l], router_bias[l], w_expert_gate[l], w_expert_up[l],
                        w_expert_down[l], w_shared_gate[l], w_shared_up[l], w_shared_down[l])
        x = layer_norm(ALPHA * x + ffn, ln2_g[l], ln2_b[l])
    return x
```

```python
import functools

import jax
import jax.numpy as jnp
from jax import lax
from jax.experimental import pallas as pl
from jax.experimental.pallas import tpu as pltpu

F32 = jnp.float32
BF16 = jnp.bfloat16

D_MODEL = 2048
DEPTH = 2
CHUNK = 64
A_WIDTH = 1024
A_BLOCK = 128
N_HEAD_GROUPS = 4
HEAD = A_WIDTH // N_HEAD_GROUPS
CONV_W = 3
N_BRANCH = 3
N_EXPERTS = 64
TOP_K = 8
N_GROUPS = 8
GROUP_SIZE = N_EXPERTS // N_GROUPS
TOPK_GROUPS = 4
EXPERT_HIDDEN = 512
ROUTED_SCALE = 2.5
ALPHA = (2.0 * DEPTH) ** 0.25
LN_EPS = 1e-5

MERGE_COLS = 512
VMEM_LIMIT = 56 * 1024 * 1024


def _params(sem, vmem=VMEM_LIMIT):
    return pltpu.CompilerParams(dimension_semantics=sem, vmem_limit_bytes=vmem)


def _gelu(x):
    return 0.5 * x * (1.0 + lax.erf(x * (0.5 ** 0.5)))


def _layer_norm_rows(r, g, b):
    mu = jnp.mean(r, axis=-1, keepdims=True)
    c = r - mu
    var = jnp.mean(c * c, axis=-1, keepdims=True)
    return c * lax.rsqrt(var + LN_EPS) * g + b


def _ln_kernel(*refs, n_parts, scale0):
    parts = refs[:n_parts]
    g_ref, b_ref, o32_ref, o16_ref = refs[n_parts:]
    r = parts[0][...] * scale0 if scale0 != 1.0 else parts[0][...]
    for p in parts[1:]:
        r = r + p[...]
    y = _layer_norm_rows(r, g_ref[...], b_ref[...])
    o32_ref[...] = y
    o16_ref[...] = y.astype(BF16)


def _ln_call(parts, g, b, scale0=1.0, tm=256):
    t, d = parts[0].shape
    row = pl.BlockSpec((tm, d), lambda i: (i, 0))
    vec = pl.BlockSpec((1, d), lambda i: (0, 0))
    return pl.pallas_call(
        functools.partial(_ln_kernel, n_parts=len(parts), scale0=scale0),
        grid=(t // tm,),
        in_specs=[row] * len(parts) + [vec, vec],
        out_specs=[row, row],
        out_shape=[jax.ShapeDtypeStruct((t, d), F32), jax.ShapeDtypeStruct((t, d), BF16)],
        compiler_params=_params(("parallel",)),
        name="layer_norm",
    )(*parts, g.reshape(1, d), b.reshape(1, d))


def _mix_tokens_kernel(x_ref, w_ref, ws_ref, bs_ref, wc_ref, wp_ref, ps_ref,
                       ya_ref, yb_ref, yc_ref, zc_ref, pc_ref):
    g = pl.program_id(0)
    i = pl.program_id(1)
    tm = x_ref.shape[0]

    @pl.when(i == 0)
    def _():
        zc_ref[...] = jnp.zeros_like(zc_ref)
        pc_ref[...] = jnp.zeros_like(pc_ref)

    proj = jnp.dot(x_ref[...], w_ref[0], preferred_element_type=F32)
    u = _gelu(proj[:, 0 * HEAD:1 * HEAD])
    v = _gelu(proj[:, 1 * HEAD:2 * HEAD])
    bg = proj[:, 2 * HEAD:3 * HEAD]
    cg = proj[:, 3 * HEAD:4 * HEAD]
    h = proj[:, 4 * HEAD:5 * HEAD]
    p = proj[:, 5 * HEAD:6 * HEAD]

    mu = jnp.mean(v, axis=-1, keepdims=True)
    vc = v - mu
    var = jnp.mean(vc * vc, axis=-1, keepdims=True)
    vn = (vc * lax.rsqrt(var + LN_EPS)).astype(BF16)
    qi = lax.broadcasted_iota(jnp.int32, (A_BLOCK, A_BLOCK), 0) // CHUNK
    kj = lax.broadcasted_iota(jnp.int32, (A_BLOCK, A_BLOCK), 1) // CHUNK
    ws = jnp.where(kj <= qi, ws_ref[0], 0.0).astype(BF16)
    bs = bs_ref[0]
    for n in range(tm // A_BLOCK):
        rows = slice(n * A_BLOCK, (n + 1) * A_BLOCK)
        mixed = jnp.dot(ws, vn[rows], preferred_element_type=F32) + bs
        ya_ref[rows, :] = (u[rows] * mixed).astype(BF16)

    z = cg * h
    row = lax.broadcasted_iota(jnp.int32, (tm, 1), 0)
    prev1 = zc_ref[7:8, :]
    prev2 = zc_ref[6:7, :]
    z1 = jnp.where(row == 0, prev1, pltpu.roll(z, 1, 0))
    z2 = jnp.where(row == 0, prev2, jnp.where(row == 1, prev1, pltpu.roll(z, 2, 0)))
    conv = wc_ref[0, 0:1, :] * z2 + wc_ref[0, 1:2, :] * z1 + wc_ref[0, 2:3, :] * z
    yb_ref[...] = (bg * conv).astype(BF16)
    zc_ref[...] = z[tm - 8:, :]

    win = lax.shift_left(jnp.int32(2), g)
    p16 = p.astype(BF16)
    pext = jnp.concatenate([pc_ref[...], p16], axis=0)
    rr = lax.broadcasted_iota(jnp.int32, (tm, tm + A_BLOCK), 0) + A_BLOCK
    cc = lax.broadcasted_iota(jnp.int32, (tm, tm + A_BLOCK), 1)
    band = jnp.where((cc <= rr) & (cc > rr - win), 1.0, 0.0).astype(BF16)
    total = jnp.dot(band, pext, preferred_element_type=F32)
    count = jnp.minimum(i * tm + row + 1, win).astype(F32)
    pooled = (total / count - p).astype(BF16)
    yc = jnp.dot(pooled, wp_ref[0], preferred_element_type=F32) * ps_ref[0]
    yc_ref[...] = yc.astype(BF16)
    pc_ref[...] = p16[tm - A_BLOCK:, :]


def _mix_tokens_call(xb, w_in_g, w_s, b_s, w_conv_g, w_pool, pool_scale_g, tm=512):
    t, d = xb.shape
    grp = lambda *shape: pl.BlockSpec((1,) + shape, lambda g, i: (g,) + (0,) * len(shape))
    out_spec = pl.BlockSpec((tm, HEAD), lambda g, i: (i, g))
    out_shape = jax.ShapeDtypeStruct((t, A_WIDTH), BF16)
    return pl.pallas_call(
        _mix_tokens_kernel,
        grid=(N_HEAD_GROUPS, t // tm),
        in_specs=[
            pl.BlockSpec((tm, d), lambda g, i: (i, 0)),
            grp(d, 6 * HEAD),
            grp(A_BLOCK, A_BLOCK),
            grp(A_BLOCK, 1),
            grp(CONV_W, HEAD),
            grp(HEAD, HEAD),
            grp(1, HEAD),
        ],
        out_specs=[out_spec, out_spec, out_spec],
        out_shape=[out_shape, out_shape, out_shape],
        scratch_shapes=[pltpu.VMEM((8, HEAD), F32), pltpu.VMEM((A_BLOCK, HEAD), BF16)],
        compiler_params=_params(("arbitrary", "arbitrary")),
        name="mix_tokens",
    )(xb, w_in_g, w_s, b_s, w_conv_g, w_pool, pool_scale_g)


def _merge_kernel(x_ref, ya_ref, yb_ref, yc_ref, wg_ref, bg_ref, wa_ref, wb_ref, wc_ref, m_ref):
    c = MERGE_COLS
    gates = jax.nn.sigmoid(jnp.dot(x_ref[...], wg_ref[0], preferred_element_type=F32) + bg_ref[0])
    m = gates[:, 0:c] * jnp.dot(ya_ref[...], wa_ref[0], preferred_element_type=F32)
    m = m + gates[:, c:2 * c] * jnp.dot(yb_ref[...], wb_ref[0], preferred_element_type=F32)
    m = m + gates[:, 2 * c:3 * c] * jnp.dot(yc_ref[...], wc_ref[0], preferred_element_type=F32)
    m_ref[...] = m.astype(BF16)


def _merge_call(xb, ya, yb, yc, wg_j, bg_j, wa_j, wb_j, wc_j, tm=512):
    t, d = xb.shape
    nj = d // MERGE_COLS
    col = lambda *shape: pl.BlockSpec((1,) + shape, lambda j, i: (j,) + (0,) * len(shape))
    tok = lambda width: pl.BlockSpec((tm, width), lambda j, i: (i, 0))
    return pl.pallas_call(
        _merge_kernel,
        grid=(nj, t // tm),
        in_specs=[tok(d), tok(A_WIDTH), tok(A_WIDTH), tok(A_WIDTH),
                  col(d, N_BRANCH * MERGE_COLS), col(1, N_BRANCH * MERGE_COLS),
                  col(A_WIDTH, MERGE_COLS), col(A_WIDTH, MERGE_COLS), col(A_WIDTH, MERGE_COLS)],
        out_specs=pl.BlockSpec((tm, MERGE_COLS), lambda j, i: (i, j)),
        out_shape=jax.ShapeDtypeStruct((t, d), BF16),
        compiler_params=_params(("arbitrary", "arbitrary")),
        name="merge_branches",
    )(xb, ya, yb, yc, wg_j, bg_j, wa_j, wb_j, wc_j)


def _out_proj_kernel(m_ref, x_ref, w_ref, g_ref, b_ref, o32_ref, o16_ref):
    r = ALPHA * x_ref[...] + jnp.dot(m_ref[...], w_ref[...], preferred_element_type=F32)
    y = _layer_norm_rows(r, g_ref[...], b_ref[...])
    o32_ref[...] = y
    o16_ref[...] = y.astype(BF16)


def _out_proj_call(merged, x, w_out, g, b, tm=512):
    t, d = x.shape
    row = pl.BlockSpec((tm, d), lambda i: (i, 0))
    vec = pl.BlockSpec((1, d), lambda i: (0, 0))
    return pl.pallas_call(
        _out_proj_kernel,
        grid=(t // tm,),
        in_specs=[row, row, pl.BlockSpec((d, d), lambda i: (0, 0)), vec, vec],
        out_specs=[row, row],
        out_shape=[jax.ShapeDtypeStruct((t, d), F32), jax.ShapeDtypeStruct((t, d), BF16)],
        compiler_params=_params(("parallel",)),
        name="out_proj_ln",
    )(merged, x, w_out, g.reshape(1, d), b.reshape(1, d))


def _lane_max(v):
    return jnp.max(v, axis=-1, keepdims=True)


def _first_lane_of(v, m, lane, sentinel):
    return jnp.min(jnp.where(v == m, lane, sentinel), axis=-1, keepdims=True)


def _router_kernel(x_ref, w_ref, b_ref, comb_ref):
    tm = x_ref.shape[0]
    neg = -jnp.inf
    logits = jnp.dot(x_ref[...], w_ref[...], preferred_element_type=F32,
                     precision=lax.Precision.HIGHEST)
    scores = jax.nn.sigmoid(logits)
    sel = scores + b_ref[...]
    lane = lax.broadcasted_iota(jnp.int32, (tm, N_EXPERTS), 1)
    grp = lane // GROUP_SIZE

    gscore = jnp.zeros((tm, N_EXPERTS), F32)
    for gi in range(N_GROUPS):
        inside = grp == gi
        v = jnp.where(inside, sel, neg)
        m1 = _lane_max(v)
        i1 = _first_lane_of(v, m1, lane, N_EXPERTS)
        m2 = _lane_max(jnp.where(lane == i1, neg, v))
        gscore = jnp.where(inside, m1 + m2, gscore)

    keep = jnp.zeros((tm, N_EXPERTS), jnp.bool_)
    for _ in range(TOPK_GROUPS):
        m = _lane_max(gscore)
        gsel = _first_lane_of(gscore, m, grp, N_GROUPS)
        hit = grp == gsel
        keep = keep | hit
        gscore = jnp.where(hit, neg, gscore)

    cand = jnp.where(keep, sel, neg)
    picked = jnp.zeros((tm, N_EXPERTS), jnp.bool_)
    for _ in range(TOP_K):
        m = _lane_max(cand)
        idx = _first_lane_of(cand, m, lane, N_EXPERTS)
        hit = lane == idx
        picked = picked | hit
        cand = jnp.where(hit, neg, cand)

    w = jnp.where(picked, scores, 0.0)
    comb_ref[...] = w / jnp.sum(w, axis=-1, keepdims=True) * ROUTED_SCALE


def _router_call(x, w_router, router_bias, tm=128):
    t, d = x.shape
    return pl.pallas_call(
        _router_kernel,
        grid=(t // tm,),
        in_specs=[pl.BlockSpec((tm, d), lambda i: (i, 0)),
                  pl.BlockSpec((d, N_EXPERTS), lambda i: (0, 0)),
                  pl.BlockSpec((1, N_EXPERTS), lambda i: (0, 0))],
        out_specs=pl.BlockSpec((tm, N_EXPERTS), lambda i: (i, 0)),
        out_shape=jax.ShapeDtypeStruct((t, N_EXPERTS), F32),
        compiler_params=_params(("parallel",)),
        name="router",
    )(x, w_router, router_bias.reshape(1, N_EXPERTS))


def _shared_kernel(x32_ref, x16_ref, wg_ref, wu_ref, wd_ref, o_ref):
    xb = x16_ref[...]
    a = jnp.dot(xb, wg_ref[...], preferred_element_type=F32)
    b = jnp.dot(xb, wu_ref[...], preferred_element_type=F32)
    hid = (jax.nn.silu(a) * b).astype(BF16)
    o_ref[...] = ALPHA * x32_ref[...] + jnp.dot(hid, wd_ref[...], preferred_element_type=F32)


def _shared_call(x32, x16, wg, wu, wd, tm=512):
    t, d = x32.shape
    hdim = wg.shape[1]
    row = pl.BlockSpec((tm, d), lambda i: (i, 0))
    return pl.pallas_call(
        _shared_kernel,
        grid=(t // tm,),
        in_specs=[row, row,
                  pl.BlockSpec((d, hdim), lambda i: (0, 0)),
                  pl.BlockSpec((d, hdim), lambda i: (0, 0)),
                  pl.BlockSpec((hdim, d), lambda i: (0, 0))],
        out_specs=row,
        out_shape=jax.ShapeDtypeStruct((t, d), F32),
        compiler_params=_params(("parallel",)),
        name="shared_expert",
    )(x32, x16, wg, wu, wd)


def _dense_moe_kernel(x_ref, s_ref, comb_ref, w1_ref, w3_ref, w2_ref, g_ref, b_ref,
                      o32_ref, o16_ref, acc_ref):
    e = pl.program_id(1)

    @pl.when(e == 0)
    def _():
        acc_ref[...] = s_ref[...]

    xb = x_ref[...]
    a = jnp.dot(xb, w1_ref[0].astype(BF16), preferred_element_type=F32)
    b = jnp.dot(xb, w3_ref[0].astype(BF16), preferred_element_type=F32)
    hid = (jax.nn.silu(a) * b).astype(BF16)
    y = jnp.dot(hid, w2_ref[0].astype(BF16), preferred_element_type=F32)
    lane = lax.broadcasted_iota(jnp.int32, comb_ref.shape, 1)
    ce = jnp.sum(jnp.where(lane == e, comb_ref[...], 0.0), axis=-1, keepdims=True)
    acc_ref[...] += ce * y

    @pl.when(e == pl.num_programs(1) - 1)
    def _():
        out = _layer_norm_rows(acc_ref[...], g_ref[...], b_ref[...])
        o32_ref[...] = out
        o16_ref[...] = out.astype(BF16)


def _dense_moe_call(x16, s, comb, w1, w3, w2, g, b, tm=256):
    t, d = x16.shape
    ne, _, hdim = w1.shape
    row = pl.BlockSpec((tm, d), lambda i, e: (i, 0))
    vec = pl.BlockSpec((1, d), lambda i, e: (0, 0))
    return pl.pallas_call(
        _dense_moe_kernel,
        grid=(t // tm, ne),
        in_specs=[row, row,
                  pl.BlockSpec((tm, ne), lambda i, e: (i, 0)),
                  pl.BlockSpec((1, d, hdim), lambda i, e: (e, 0, 0)),
                  pl.BlockSpec((1, d, hdim), lambda i, e: (e, 0, 0)),
                  pl.BlockSpec((1, hdim, d), lambda i, e: (e, 0, 0)),
                  vec, vec],
        out_specs=[row, row],
        out_shape=[jax.ShapeDtypeStruct((t, d), F32), jax.ShapeDtypeStruct((t, d), BF16)],
        scratch_shapes=[pltpu.VMEM((tm, d), F32)],
        compiler_params=_params(("arbitrary", "arbitrary")),
        name="dense_moe",
    )(x16, s, comb, w1, w3, w2, g.reshape(1, d), b.reshape(1, d))


def _mixer_weights(w_in, w_conv, pool_scale, w_gate, b_gate, w_proj_a, w_proj_b, w_proj_c, w_out):
    d = D_MODEL
    nj = d // MERGE_COLS
    w_in_g = w_in.reshape(d, 6, N_HEAD_GROUPS, HEAD).transpose(2, 0, 1, 3)
    w_in_g = w_in_g.reshape(N_HEAD_GROUPS, d, 6 * HEAD).astype(BF16)
    w_conv_g = w_conv.reshape(CONV_W, N_HEAD_GROUPS, HEAD).transpose(1, 0, 2)
    pool_scale_g = pool_scale.reshape(N_HEAD_GROUPS, 1, HEAD)
    wg_j = w_gate.reshape(d, N_BRANCH, nj, MERGE_COLS).transpose(2, 0, 1, 3)
    wg_j = wg_j.reshape(nj, d, N_BRANCH * MERGE_COLS).astype(BF16)
    bg_j = b_gate.reshape(N_BRANCH, nj, MERGE_COLS).transpose(1, 0, 2).reshape(nj, 1, N_BRANCH * MERGE_COLS)
    split = lambda w: w.reshape(A_WIDTH, nj, MERGE_COLS).transpose(1, 0, 2).astype(BF16)
    return (w_in_g, w_conv_g, pool_scale_g, wg_j, bg_j,
            split(w_proj_a), split(w_proj_b), split(w_proj_c), w_out.astype(BF16))


def kernel(x, in_ln_g, in_ln_b, w_in, w_spatial, b_spatial, w_conv, w_pool, pool_scale,
           w_gate, b_gate, w_proj_a, w_proj_b, w_proj_c, w_out, ln1_g, ln1_b,
           w_router, router_bias, w_expert_gate, w_expert_up, w_expert_down,
           w_shared_gate, w_shared_up, w_shared_down, ln2_g, ln2_b):
    bsz, seq, d = x.shape
    x32, x16 = _ln_call([x.reshape(bsz * seq, d)], in_ln_g, in_ln_b)
    for l in range(DEPTH):
        (w_in_g, w_conv_g, pool_scale_g, wg_j, bg_j, wa_j, wb_j, wc_j, w_out16) = _mixer_weights(
            w_in[l], w_conv[l], pool_scale[l], w_gate[l], b_gate[l],
            w_proj_a[l], w_proj_b[l], w_proj_c[l], w_out[l])
        ya, yb, yc = _mix_tokens_call(x16, w_in_g, w_spatial[l],
                                      b_spatial[l].reshape(N_HEAD_GROUPS, A_BLOCK, 1),
                                      w_conv_g, w_pool[l].astype(BF16), pool_scale_g)
        merged = _merge_call(x16, ya, yb, yc, wg_j, bg_j, wa_j, wb_j, wc_j)
        x32, x16 = _out_proj_call(merged, x32, w_out16, ln1_g[l], ln1_b[l])
        comb = _router_call(x32, w_router[l], router_bias[l])
        s = _shared_call(x32, x16, w_shared_gate[l].astype(BF16), w_shared_up[l].astype(BF16),
                         w_shared_down[l].astype(BF16))
        x32, x16 = _dense_moe_call(x16, s, comb, w_expert_gate[l], w_expert_up[l],
                                   w_expert_down[l], ln2_g[l], ln2_b[l])
    return x32.reshape(bsz, seq, d)
```

```python
import functools

import jax
import jax.numpy as jnp
from jax import lax
from jax.experimental import pallas as pl
from jax.experimental.pallas import tpu as pltpu

F32 = jnp.float32
BF16 = jnp.bfloat16
I32 = jnp.int32

D_MODEL = 2048
DEPTH = 2
CHUNK = 64
A_WIDTH = 1024
A_BLOCK = 128
N_HEAD_GROUPS = 4
HEAD = A_WIDTH // N_HEAD_GROUPS
N_STREAMS = 6
CONV_W = 3
N_BRANCH = 3
N_EXPERTS = 64
TOP_K = 8
N_GROUPS = 8
GROUP_SIZE = N_EXPERTS // N_GROUPS
TOPK_GROUPS = 4
ROUTED_SCALE = 2.5
ALPHA = (2.0 * DEPTH) ** 0.25
LN_EPS = 1e-5

LANES = 128
ROW_TILES = D_MODEL // LANES
MERGE_COLS = 512
WINDOW = 128
PAIRS = WINDOW * TOP_K
EXPERT_TILE = 256
VMEM_LIMIT = 56 * 1024 * 1024


def _params(sem, vmem=VMEM_LIMIT):
    return pltpu.CompilerParams(dimension_semantics=sem, vmem_limit_bytes=vmem)


def _gelu(x):
    return 0.5 * x * (1.0 + lax.erf(x * (0.5 ** 0.5)))


def _layer_norm_rows(r, g, b):
    mu = jnp.mean(r, axis=-1, keepdims=True)
    c = r - mu
    var = jnp.mean(c * c, axis=-1, keepdims=True)
    return c * lax.rsqrt(var + LN_EPS) * g + b


def _ln_kernel(x_ref, g_ref, b_ref, o32_ref, o16_ref):
    y = _layer_norm_rows(x_ref[...], g_ref[...], b_ref[...])
    o32_ref[...] = y
    o16_ref[...] = y.astype(BF16)


def _ln_call(x, g, b, tm=256):
    t, d = x.shape
    row = pl.BlockSpec((tm, d), lambda i: (i, 0))
    vec = pl.BlockSpec((1, d), lambda i: (0, 0))
    return pl.pallas_call(
        _ln_kernel,
        grid=(t // tm,),
        in_specs=[row, vec, vec],
        out_specs=[row, row],
        out_shape=[jax.ShapeDtypeStruct((t, d), F32), jax.ShapeDtypeStruct((t, d), BF16)],
        compiler_params=_params(("parallel",)),
        name="layer_norm",
    )(x, g.reshape(1, d), b.reshape(1, d))


def _mix_tokens_kernel(x_ref, wu_ref, wv_ref, wbg_ref, wcg_ref, wh_ref, wpp_ref,
                       ws_ref, bs_ref, wc_ref, wp_ref, ps_ref,
                       ya_ref, yb_ref, yc_ref, zc_ref, pc_ref):
    g = pl.program_id(0)
    i = pl.program_id(1)
    tm = x_ref.shape[0]

    @pl.when(i == 0)
    def _():
        zc_ref[...] = jnp.zeros_like(zc_ref)
        pc_ref[...] = jnp.zeros_like(pc_ref)

    x = x_ref[...]
    proj = lambda w_ref: jnp.dot(x, w_ref[...], preferred_element_type=F32)

    u = _gelu(proj(wu_ref))
    v = _gelu(proj(wv_ref))
    mu = jnp.mean(v, axis=-1, keepdims=True)
    vc = v - mu
    var = jnp.mean(vc * vc, axis=-1, keepdims=True)
    vn = (vc * lax.rsqrt(var + LN_EPS)).astype(BF16)
    qi = lax.broadcasted_iota(I32, (A_BLOCK, A_BLOCK), 0) // CHUNK
    kj = lax.broadcasted_iota(I32, (A_BLOCK, A_BLOCK), 1) // CHUNK
    ws = jnp.where(kj <= qi, ws_ref[0], 0.0).astype(BF16)
    bs = bs_ref[0]
    for n in range(tm // A_BLOCK):
        rows = slice(n * A_BLOCK, (n + 1) * A_BLOCK)
        mixed = jnp.dot(ws, vn[rows], preferred_element_type=F32) + bs
        ya_ref[rows, :] = (u[rows] * mixed).astype(BF16)

    z = proj(wcg_ref) * proj(wh_ref)
    row = lax.broadcasted_iota(I32, (tm, 1), 0)
    prev1 = zc_ref[7:8, :]
    prev2 = zc_ref[6:7, :]
    z1 = jnp.where(row == 0, prev1, pltpu.roll(z, 1, 0))
    z2 = jnp.where(row == 0, prev2, jnp.where(row == 1, prev1, pltpu.roll(z, 2, 0)))
    conv = wc_ref[0:1, :] * z2 + wc_ref[1:2, :] * z1 + wc_ref[2:3, :] * z
    yb_ref[...] = (proj(wbg_ref) * conv).astype(BF16)
    zc_ref[...] = z[tm - 8:, :]

    p = proj(wpp_ref)
    win = lax.shift_left(jnp.int32(2), g)
    p16 = p.astype(BF16)
    pext = jnp.concatenate([pc_ref[...], p16], axis=0)
    rr = lax.broadcasted_iota(I32, (tm, tm + A_BLOCK), 0) + A_BLOCK
    cc = lax.broadcasted_iota(I32, (tm, tm + A_BLOCK), 1)
    band = jnp.where((cc <= rr) & (cc > rr - win), 1.0, 0.0).astype(BF16)
    total = jnp.dot(band, pext, preferred_element_type=F32)
    count = jnp.minimum(i * tm + row + 1, win).astype(F32)
    pooled = (total / count - p).astype(BF16)
    yc = jnp.dot(pooled, wp_ref[0], preferred_element_type=F32) * ps_ref[...]
    yc_ref[...] = yc.astype(BF16)
    pc_ref[...] = p16[tm - A_BLOCK:, :]


def _mix_tokens_call(xb, w_in16, w_s, b_s, w_conv, w_pool16, pool_scale, tm=512):
    t, d = xb.shape
    stream = lambda s: pl.BlockSpec((d, HEAD), lambda g, i: (0, s * N_HEAD_GROUPS + g))
    grp = lambda *shape: pl.BlockSpec((1,) + shape, lambda g, i: (g,) + (0,) * len(shape))
    out_spec = pl.BlockSpec((tm, HEAD), lambda g, i: (i, g))
    out_shape = jax.ShapeDtypeStruct((t, A_WIDTH), BF16)
    return pl.pallas_call(
        _mix_tokens_kernel,
        grid=(N_HEAD_GROUPS, t // tm),
        in_specs=[pl.BlockSpec((tm, d), lambda g, i: (i, 0))]
        + [stream(s) for s in range(N_STREAMS)]
        + [grp(A_BLOCK, A_BLOCK), grp(A_BLOCK, 1),
           pl.BlockSpec((CONV_W, HEAD), lambda g, i: (0, g)),
           grp(HEAD, HEAD),
           pl.BlockSpec((1, HEAD), lambda g, i: (0, g))],
        out_specs=[out_spec, out_spec, out_spec],
        out_shape=[out_shape, out_shape, out_shape],
        scratch_shapes=[pltpu.VMEM((8, HEAD), F32), pltpu.VMEM((A_BLOCK, HEAD), BF16)],
        compiler_params=_params(("arbitrary", "arbitrary")),
        name="mix_tokens",
    )(xb, *([w_in16] * N_STREAMS), w_s, b_s.reshape(N_HEAD_GROUPS, A_BLOCK, 1), w_conv,
      w_pool16, pool_scale.reshape(1, A_WIDTH))


def _merge_kernel(x_ref, ya_ref, yb_ref, yc_ref, wg0_ref, wg1_ref, wg2_ref, bg0_ref, bg1_ref, bg2_ref,
                  wa_ref, wb_ref, wc_ref, m_ref):
    x = x_ref[...]
    gate = lambda w_ref, b_ref: jax.nn.sigmoid(
        jnp.dot(x, w_ref[...], preferred_element_type=F32) + b_ref[...])
    m = gate(wg0_ref, bg0_ref) * jnp.dot(ya_ref[...], wa_ref[...], preferred_element_type=F32)
    m = m + gate(wg1_ref, bg1_ref) * jnp.dot(yb_ref[...], wb_ref[...], preferred_element_type=F32)
    m = m + gate(wg2_ref, bg2_ref) * jnp.dot(yc_ref[...], wc_ref[...], preferred_element_type=F32)
    m_ref[...] = m.astype(BF16)


def _merge_call(xb, ya, yb, yc, w_gate16, b_gate, wa16, wb16, wc16, tm=512):
    t, d = xb.shape
    nj = d // MERGE_COLS
    tok = lambda width: pl.BlockSpec((tm, width), lambda j, i: (i, 0))
    gate_w = lambda br: pl.BlockSpec((d, MERGE_COLS), lambda j, i: (0, br * nj + j))
    gate_b = lambda br: pl.BlockSpec((1, MERGE_COLS), lambda j, i: (0, br * nj + j))
    branch_w = pl.BlockSpec((A_WIDTH, MERGE_COLS), lambda j, i: (0, j))
    b2 = b_gate.reshape(1, N_BRANCH * d)
    return pl.pallas_call(
        _merge_kernel,
        grid=(nj, t // tm),
        in_specs=[tok(d), tok(A_WIDTH), tok(A_WIDTH), tok(A_WIDTH)]
        + [gate_w(br) for br in range(N_BRANCH)] + [gate_b(br) for br in range(N_BRANCH)]
        + [branch_w] * N_BRANCH,
        out_specs=pl.BlockSpec((tm, MERGE_COLS), lambda j, i: (i, j)),
        out_shape=jax.ShapeDtypeStruct((t, d), BF16),
        compiler_params=_params(("arbitrary", "arbitrary")),
        name="merge_branches",
    )(xb, ya, yb, yc, w_gate16, w_gate16, w_gate16, b2, b2, b2, wa16, wb16, wc16)


def _out_proj_kernel(m_ref, x_ref, w_ref, g_ref, b_ref, o32_ref, o16_ref):
    r = ALPHA * x_ref[...] + jnp.dot(m_ref[...], w_ref[...], preferred_element_type=F32)
    y = _layer_norm_rows(r, g_ref[...], b_ref[...])
    o32_ref[...] = y
    o16_ref[...] = y.astype(BF16)


def _out_proj_call(merged, x, w_out16, g, b, tm=512):
    t, d = x.shape
    row = pl.BlockSpec((tm, d), lambda i: (i, 0))
    vec = pl.BlockSpec((1, d), lambda i: (0, 0))
    return pl.pallas_call(
        _out_proj_kernel,
        grid=(t // tm,),
        in_specs=[row, row, pl.BlockSpec((d, d), lambda i: (0, 0)), vec, vec],
        out_specs=[row, row],
        out_shape=[jax.ShapeDtypeStruct((t, d), F32), jax.ShapeDtypeStruct((t, d), BF16)],
        compiler_params=_params(("parallel",)),
        name="out_proj_ln",
    )(merged, x, w_out16, g.reshape(1, d), b.reshape(1, d))


def _lane_max(v):
    return jnp.max(v, axis=-1, keepdims=True)


def _first_lane_of(v, m, lane, sentinel):
    return jnp.min(jnp.where(v == m, lane, sentinel), axis=-1, keepdims=True)


def _router_kernel(x_ref, w_ref, b_ref, wts_ref, pos_ref, cnt_ref, start_ref):
    tm = x_ref.shape[0]
    neg = -jnp.inf
    logits = jnp.dot(x_ref[...], w_ref[...], preferred_element_type=F32,
                     precision=lax.Precision.HIGHEST)
    scores = jax.nn.sigmoid(logits)
    sel = scores + b_ref[...]
    lane = lax.broadcasted_iota(I32, (tm, N_EXPERTS), 1)
    grp = lane // GROUP_SIZE

    gscore = jnp.zeros((tm, N_EXPERTS), F32)
    for gi in range(N_GROUPS):
        inside = grp == gi
        v = jnp.where(inside, sel, neg)
        m1 = _lane_max(v)
        i1 = _first_lane_of(v, m1, lane, N_EXPERTS)
        m2 = _lane_max(jnp.where(lane == i1, neg, v))
        gscore = jnp.where(inside, m1 + m2, gscore)

    keep = jnp.zeros((tm, N_EXPERTS), jnp.bool_)
    for _ in range(TOPK_GROUPS):
        m = _lane_max(gscore)
        gsel = _first_lane_of(gscore, m, grp, N_GROUPS)
        hit = grp == gsel
        keep = keep | hit
        gscore = jnp.where(hit, neg, gscore)

    cand = jnp.where(keep, sel, neg)
    hits = []
    for _ in range(TOP_K):
        m = _lane_max(cand)
        idx = _first_lane_of(cand, m, lane, N_EXPERTS)
        hit = lane == idx
        hits.append(hit)
        cand = jnp.where(hit, neg, cand)

    picked = functools.reduce(jnp.logical_or, hits)
    w = [jnp.sum(jnp.where(hit, scores, 0.0), axis=-1, keepdims=True) for hit in hits]
    wts_ref[...] = jnp.concatenate(w, axis=1) / functools.reduce(jnp.add, w) * ROUTED_SCALE

    onehot = jnp.where(picked, 1.0, 0.0)
    r_i = lax.broadcasted_iota(I32, (tm, tm), 0)
    c_i = lax.broadcasted_iota(I32, (tm, tm), 1)
    earlier = jnp.where(c_i < r_i, 1.0, 0.0).astype(BF16)
    rank = jnp.dot(earlier, onehot.astype(BF16), preferred_element_type=F32)
    count = jnp.sum(onehot, axis=0, keepdims=True)
    e_r = lax.broadcasted_iota(I32, (N_EXPERTS, N_EXPERTS), 0)
    e_c = lax.broadcasted_iota(I32, (N_EXPERTS, N_EXPERTS), 1)
    before = jnp.where(e_r < e_c, 1.0, 0.0).astype(BF16)
    start = jnp.dot(jnp.broadcast_to(count, (8, N_EXPERTS)).astype(BF16), before,
                    preferred_element_type=F32)[0:1, :]
    where_to = rank + start
    pos = [jnp.sum(jnp.where(hit, where_to, 0.0), axis=-1, keepdims=True) for hit in hits]
    pos_ref[...] = jnp.concatenate(pos, axis=1).astype(I32)
    cnt_ref[0] = count.astype(I32)
    start_ref[0] = start.astype(I32)


def _router_call(x, w_router, router_bias):
    t, d = x.shape
    nw = t // WINDOW
    per_tok = pl.BlockSpec((WINDOW, TOP_K), lambda i: (i, 0))
    per_win = pl.BlockSpec((1, 1, N_EXPERTS), lambda i: (i, 0, 0))
    return pl.pallas_call(
        _router_kernel,
        grid=(nw,),
        in_specs=[pl.BlockSpec((WINDOW, d), lambda i: (i, 0)),
                  pl.BlockSpec((d, N_EXPERTS), lambda i: (0, 0)),
                  pl.BlockSpec((1, N_EXPERTS), lambda i: (0, 0))],
        out_specs=[per_tok, per_tok, per_win, per_win],
        out_shape=[jax.ShapeDtypeStruct((t, TOP_K), F32), jax.ShapeDtypeStruct((t, TOP_K), I32),
                   jax.ShapeDtypeStruct((nw, 1, N_EXPERTS), I32),
                   jax.ShapeDtypeStruct((nw, 1, N_EXPERTS), I32)],
        compiler_params=_params(("parallel",)),
        name="router",
    )(x, w_router, router_bias.reshape(1, N_EXPERTS))


def _shared_kernel(x32_ref, x16_ref, wg_ref, wu_ref, wd_ref, o_ref):
    xb = x16_ref[...]
    a = jnp.dot(xb, wg_ref[...], preferred_element_type=F32)
    b = jnp.dot(xb, wu_ref[...], preferred_element_type=F32)
    hid = (jax.nn.silu(a) * b).astype(BF16)
    o_ref[...] = ALPHA * x32_ref[...] + jnp.dot(hid, wd_ref[...], preferred_element_type=F32)


def _shared_call(x32, x16, wg, wu, wd, tm=512):
    t, d = x32.shape
    hdim = wg.shape[1]
    row = pl.BlockSpec((tm, d), lambda i: (i, 0))
    return pl.pallas_call(
        _shared_kernel,
        grid=(t // tm,),
        in_specs=[row, row,
                  pl.BlockSpec((d, hdim), lambda i: (0, 0)),
                  pl.BlockSpec((d, hdim), lambda i: (0, 0)),
                  pl.BlockSpec((hdim, d), lambda i: (0, 0))],
        out_specs=row,
        out_shape=jax.ShapeDtypeStruct((t, d), F32),
        compiler_params=_params(("parallel",)),
        name="shared_expert",
    )(x32, x16, wg, wu, wd)


def _segment_copies(cnt_ref, start_ref, off_ref, win, local, remote, sem, to_remote, act):
    def body(e, carry):
        n = cnt_ref[win * N_EXPERTS + e]

        @pl.when(n > 0)
        def _():
            loc = local.at[pl.ds(start_ref[win * N_EXPERTS + e], n)]
            rem = remote.at[pl.ds(off_ref[win * N_EXPERTS + e], n)]
            act(pltpu.make_async_copy(loc, rem, sem) if to_remote
                else pltpu.make_async_copy(rem, loc, sem))
        return carry
    lax.fori_loop(0, N_EXPERTS, body, 0)


def _pad_copies(pad_start_ref, pad_len_ref, zeros, remote, sem, act):
    def body(e, carry):
        n = pad_len_ref[e]

        @pl.when(n > 0)
        def _():
            act(pltpu.make_async_copy(zeros.at[pl.ds(0, n)], remote.at[pl.ds(pad_start_ref[e], n)], sem))
        return carry
    lax.fori_loop(0, N_EXPERTS, body, 0)


def _dispatch_kernel(cnt_ref, start_ref, off_ref, pad_start_ref, pad_len_ref,
                     pos_ref, x_ref, rows_hbm, buf, zeros, sem, pad_sem):
    w = pl.program_id(0)
    last = pl.num_programs(0) - 1
    slot = w % 2
    start = lambda c: c.start()
    wait = lambda c: c.wait()
    copies = functools.partial(_segment_copies, cnt_ref, start_ref, off_ref,
                               remote=rows_hbm, to_remote=True)

    @pl.when(w == 0)
    def _():
        zeros[...] = jnp.zeros_like(zeros)
        _pad_copies(pad_start_ref, pad_len_ref, zeros, rows_hbm, pad_sem, start)

    @pl.when(w >= 2)
    def _():
        copies(win=w - 2, local=buf.at[slot], sem=sem.at[slot], act=wait)

    def place(t, carry):
        v = x_ref[t]
        for k in range(TOP_K):
            buf[slot, pos_ref[0, 0, t * TOP_K + k]] = v
        return carry
    lax.fori_loop(0, WINDOW, place, 0)
    copies(win=w, local=buf.at[slot], sem=sem.at[slot], act=start)

    @pl.when(w == last)
    def _():
        @pl.when(w >= 1)
        def _():
            copies(win=w - 1, local=buf.at[1 - slot], sem=sem.at[1 - slot], act=wait)
        copies(win=w, local=buf.at[slot], sem=sem.at[slot], act=wait)
        _pad_copies(pad_start_ref, pad_len_ref, zeros, rows_hbm, pad_sem, wait)


def _dispatch_call(x_tiles, pos, cnt, start, off, pad_start, pad_len, n_rows):
    t = x_tiles.shape[0]
    nw = t // WINDOW
    grid_spec = pltpu.PrefetchScalarGridSpec(
        num_scalar_prefetch=5,
        grid=(nw,),
        in_specs=[pl.BlockSpec((1, 1, PAIRS), lambda w, *_: (w, 0, 0), memory_space=pltpu.SMEM),
                  pl.BlockSpec((WINDOW, ROW_TILES, LANES), lambda w, *_: (w, 0, 0))],
        out_specs=pl.BlockSpec(memory_space=pl.ANY),
        scratch_shapes=[pltpu.VMEM((2, PAIRS, ROW_TILES, LANES), F32),
                        pltpu.VMEM((EXPERT_TILE, ROW_TILES, LANES), F32),
                        pltpu.SemaphoreType.DMA((2,)),
                        pltpu.SemaphoreType.DMA(())],
    )
    return pl.pallas_call(
        _dispatch_kernel,
        grid_spec=grid_spec,
        out_shape=jax.ShapeDtypeStruct((n_rows, ROW_TILES, LANES), F32),
        compiler_params=_params(("arbitrary",)),
        name="dispatch",
    )(cnt, start, off, pad_start, pad_len, pos, x_tiles)


def _expert_kernel(tile_expert_ref, tile_block_ref, tile_first_ref, tile_valid_ref,
                   rows_ref, w1_ref, w3_ref, w2_ref, out_ref, w1b, w3b, w2b, xt):
    i = pl.program_id(0)
    tr = xt.shape[0]

    @pl.when(tile_valid_ref[i] == 1)
    def _():
        @pl.when(tile_first_ref[i] == 1)
        def _():
            w1b[...] = w1_ref[0].astype(BF16)
            w3b[...] = w3_ref[0].astype(BF16)
            w2b[...] = w2_ref[0].astype(BF16)

        for c in range(ROW_TILES):
            xt[:, c * LANES:(c + 1) * LANES] = rows_ref[pl.ds(c, tr, stride=ROW_TILES), :].astype(BF16)
        x = xt[...]
        a = jnp.dot(x, w1b[...], preferred_element_type=F32)
        b = jnp.dot(x, w3b[...], preferred_element_type=F32)
        hid = (jax.nn.silu(a) * b).astype(BF16)
        y = jnp.dot(hid, w2b[...], preferred_element_type=F32)
        for c in range(ROW_TILES):
            out_ref[pl.ds(c, tr, stride=ROW_TILES), :] = y[:, c * LANES:(c + 1) * LANES]


def _expert_call(rows2d, tile_expert, tile_block, tile_first, tile_valid, w1, w3, w2):
    n_tiles = tile_expert.shape[0]
    _, d, hdim = w1.shape
    blk = EXPERT_TILE * ROW_TILES
    row_spec = pl.BlockSpec((blk, LANES), lambda i, te, tb, tf, tv: (tb[i], 0))
    grid_spec = pltpu.PrefetchScalarGridSpec(
        num_scalar_prefetch=4,
        grid=(n_tiles,),
        in_specs=[row_spec,
                  pl.BlockSpec((1, d, hdim), lambda i, te, tb, tf, tv: (te[i], 0, 0)),
                  pl.BlockSpec((1, d, hdim), lambda i, te, tb, tf, tv: (te[i], 0, 0)),
                  pl.BlockSpec((1, hdim, d), lambda i, te, tb, tf, tv: (te[i], 0, 0))],
        out_specs=row_spec,
        scratch_shapes=[pltpu.VMEM((d, hdim), BF16), pltpu.VMEM((d, hdim), BF16),
                        pltpu.VMEM((hdim, d), BF16), pltpu.VMEM((EXPERT_TILE, d), BF16)],
    )
    return pl.pallas_call(
        _expert_kernel,
        grid_spec=grid_spec,
        out_shape=jax.ShapeDtypeStruct(rows2d.shape, F32),
        compiler_params=_params(("arbitrary",)),
        name="routed_experts",
    )(tile_expert, tile_block, tile_first, tile_valid, rows2d, w1, w3, w2)


def _combine_kernel(cnt_ref, start_ref, off_ref, pos_ref, wts_ref, rows_hbm, s_ref, g_ref, b_ref,
                    o32_ref, o16_ref, buf, acc, sem):
    w = pl.program_id(0)
    last = pl.num_programs(0) - 1
    slot = w % 2
    start = lambda c: c.start()
    wait = lambda c: c.wait()
    copies = functools.partial(_segment_copies, cnt_ref, start_ref, off_ref,
                               remote=rows_hbm, to_remote=False)

    @pl.when(w == 0)
    def _():
        copies(win=w, local=buf.at[slot], sem=sem.at[slot], act=start)

    @pl.when(w < last)
    def _():
        copies(win=w + 1, local=buf.at[1 - slot], sem=sem.at[1 - slot], act=start)

    copies(win=w, local=buf.at[slot], sem=sem.at[slot], act=wait)

    def token(t, carry):
        total = wts_ref[0, 0, t * TOP_K] * buf[slot, pos_ref[0, 0, t * TOP_K]]
        for k in range(1, TOP_K):
            total = total + wts_ref[0, 0, t * TOP_K + k] * buf[slot, pos_ref[0, 0, t * TOP_K + k]]
        acc[pl.ds(pl.multiple_of(t * ROW_TILES, ROW_TILES), ROW_TILES), :] = total
        return carry
    lax.fori_loop(0, WINDOW, token, 0)

    routed = jnp.concatenate(
        [acc[pl.ds(c, WINDOW, stride=ROW_TILES), :] for c in range(ROW_TILES)], axis=1)
    out = _layer_norm_rows(s_ref[...] + routed, g_ref[...], b_ref[...])
    o32_ref[...] = out
    o16_ref[...] = out.astype(BF16)


def _combine_call(rows_tiles, pos, wts, cnt, start, off, s, g, b):
    t, d = s.shape
    nw = t // WINDOW
    smem_block = pl.BlockSpec((1, 1, PAIRS), lambda w, *_: (w, 0, 0), memory_space=pltpu.SMEM)
    row = pl.BlockSpec((WINDOW, d), lambda w, *_: (w, 0))
    vec = pl.BlockSpec((1, d), lambda w, *_: (0, 0))
    grid_spec = pltpu.PrefetchScalarGridSpec(
        num_scalar_prefetch=3,
        grid=(nw,),
        in_specs=[smem_block, smem_block, pl.BlockSpec(memory_space=pl.ANY), row, vec, vec],
        out_specs=[row, row],
        scratch_shapes=[pltpu.VMEM((2, PAIRS, ROW_TILES, LANES), F32),
                        pltpu.VMEM((WINDOW * ROW_TILES, LANES), F32),
                        pltpu.SemaphoreType.DMA((2,))],
    )
    return pl.pallas_call(
        _combine_kernel,
        grid_spec=grid_spec,
        out_shape=[jax.ShapeDtypeStruct((t, d), F32), jax.ShapeDtypeStruct((t, d), BF16)],
        compiler_params=_params(("arbitrary",)),
        name="combine_ln",
    )(cnt, start, off, pos, wts, rows_tiles, s, g.reshape(1, d), b.reshape(1, d))


def _row_layout(cnt, n_tiles):
    cnt2 = cnt.reshape(-1, N_EXPERTS)
    total = jnp.sum(cnt2, axis=0)
    padded = (total + EXPERT_TILE - 1) // EXPERT_TILE * EXPERT_TILE
    ends = jnp.cumsum(padded)
    base = ends - padded
    off = base[None, :] + jnp.cumsum(cnt2, axis=0) - cnt2
    used = ends[-1] // EXPERT_TILE
    tile = jnp.arange(n_tiles, dtype=I32)
    tile_block = jnp.minimum(tile, used - 1)
    tile_expert = jnp.minimum(
        jnp.searchsorted(ends, tile_block * EXPERT_TILE, side="right"), N_EXPERTS - 1).astype(I32)
    tile_valid = (tile < used).astype(I32)
    tile_first = tile_valid * (tile * EXPERT_TILE == base[tile_expert]).astype(I32)
    return (off.reshape(-1).astype(I32), (base + total).astype(I32), (padded - total).astype(I32),
            tile_expert, tile_block.astype(I32), tile_first, tile_valid)


def _moe_block(x32, x16, w_router, router_bias, w1, w3, w2, ws1, ws3, ws2, g, b):
    t, d = x32.shape
    nw = t // WINDOW
    n_tiles = t * TOP_K // EXPERT_TILE + N_EXPERTS
    n_rows = n_tiles * EXPERT_TILE
    wts, pos, cnt, start = _router_call(x32, w_router, router_bias)
    off, pad_start, pad_len, tile_expert, tile_block, tile_first, tile_valid = _row_layout(cnt, n_tiles)
    cnt = cnt.reshape(-1)
    start = start.reshape(-1)
    pos = pos.reshape(nw, 1, PAIRS)
    wts = wts.reshape(nw, 1, PAIRS)
    rows = _dispatch_call(x32.reshape(t, ROW_TILES, LANES), pos, cnt, start, off, pad_start, pad_len, n_rows)
    s = _shared_call(x32, x16, ws1.astype(BF16), ws3.astype(BF16), ws2.astype(BF16))
    out_rows = _expert_call(rows.reshape(n_rows * ROW_TILES, LANES),
                            tile_expert, tile_block, tile_first, tile_valid, w1, w3, w2)
    return _combine_call(out_rows.reshape(n_rows, ROW_TILES, LANES), pos, wts, cnt, start, off, s, g, b)


def kernel(x, in_ln_g, in_ln_b, w_in, w_spatial, b_spatial, w_conv, w_pool, pool_scale,
           w_gate, b_gate, w_proj_a, w_proj_b, w_proj_c, w_out, ln1_g, ln1_b,
           w_router, router_bias, w_expert_gate, w_expert_up, w_expert_down,
           w_shared_gate, w_shared_up, w_shared_down, ln2_g, ln2_b):
    bsz, seq, d = x.shape
    x32, x16 = _ln_call(x.reshape(bsz * seq, d), in_ln_g, in_ln_b)
    for l in range(DEPTH):
        ya, yb, yc = _mix_tokens_call(x16, w_in[l].astype(BF16), w_spatial[l], b_spatial[l], w_conv[l],
                                      w_pool[l].astype(BF16), pool_scale[l])
        merged = _merge_call(x16, ya, yb, yc, w_gate[l].astype(BF16), b_gate[l],
                             w_proj_a[l].astype(BF16), w_proj_b[l].astype(BF16), w_proj_c[l].astype(BF16))
        x32, x16 = _out_proj_call(merged, x32, w_out[l].astype(BF16), ln1_g[l], ln1_b[l])
        x32, x16 = _moe_block(x32, x16, w_router[l], router_bias[l], w_expert_gate[l], w_expert_up[l],
                              w_expert_down[l], w_shared_gate[l], w_shared_up[l], w_shared_down[l],
                              ln2_g[l], ln2_b[l])
    return x32.reshape(bsz, seq, d)
```

```python
import functools

import jax
import jax.numpy as jnp
from jax import lax
from jax.experimental import pallas as pl
from jax.experimental.pallas import tpu as pltpu

F32 = jnp.float32
BF16 = jnp.bfloat16
I32 = jnp.int32
U32 = jnp.uint32

D_MODEL = 2048
DEPTH = 2
CHUNK = 64
A_WIDTH = 1024
A_BLOCK = 128
N_HEAD_GROUPS = 4
HEAD = A_WIDTH // N_HEAD_GROUPS
N_STREAMS = 6
CONV_W = 3
N_BRANCH = 3
N_EXPERTS = 64
TOP_K = 8
N_GROUPS = 8
GROUP_SIZE = N_EXPERTS // N_GROUPS
TOPK_GROUPS = 4
ROUTED_SCALE = 2.5
ALPHA = (2.0 * DEPTH) ** 0.25
LN_EPS = 1e-5

LANES = 128
ROW_TILES = D_MODEL // LANES
WORD_ROWS = ROW_TILES // 2
MERGE_COLS = 512
WINDOW = 128
PAIRS = WINDOW * TOP_K
EXPERT_TILE = 256
VMEM_LIMIT = 56 * 1024 * 1024
HIGH_HALF = 0xFFFF0000


def _params(sem, vmem=VMEM_LIMIT):
    return pltpu.CompilerParams(dimension_semantics=sem, vmem_limit_bytes=vmem)


def _gelu(x):
    return 0.5 * x * (1.0 + lax.erf(x * (0.5 ** 0.5)))


def _layer_norm_rows(r, g, b):
    mu = jnp.mean(r, axis=-1, keepdims=True)
    c = r - mu
    var = jnp.mean(c * c, axis=-1, keepdims=True)
    return c * lax.rsqrt(var + LN_EPS) * g + b


def _unpack_words(words):
    low = pltpu.bitcast(lax.shift_left(words, jnp.uint32(16)), F32)
    high = pltpu.bitcast(words & jnp.uint32(HIGH_HALF), F32)
    return low, high


def _pack_words(low, high):
    lo = pltpu.bitcast(low.astype(BF16).astype(F32), U32)
    hi = pltpu.bitcast(high.astype(BF16).astype(F32), U32)
    return lax.shift_right_logical(lo, jnp.uint32(16)) | (hi & jnp.uint32(HIGH_HALF))


def _ln_kernel(x_ref, g_ref, b_ref, o32_ref, o16_ref):
    y = _layer_norm_rows(x_ref[...], g_ref[...], b_ref[...])
    o32_ref[...] = y
    o16_ref[...] = y.astype(BF16)


def _ln_call(x, g, b, tm=256):
    t, d = x.shape
    row = pl.BlockSpec((tm, d), lambda i: (i, 0))
    vec = pl.BlockSpec((1, d), lambda i: (0, 0))
    return pl.pallas_call(
        _ln_kernel,
        grid=(t // tm,),
        in_specs=[row, vec, vec],
        out_specs=[row, row],
        out_shape=[jax.ShapeDtypeStruct((t, d), F32), jax.ShapeDtypeStruct((t, d), BF16)],
        compiler_params=_params(("parallel",)),
        name="layer_norm",
    )(x, g.reshape(1, d), b.reshape(1, d))


def _mix_tokens_kernel(x_ref, wu_ref, wv_ref, wbg_ref, wcg_ref, wh_ref, wpp_ref,
                       ws_ref, bs_ref, wc_ref, wp_ref, ps_ref,
                       ya_ref, yb_ref, yc_ref, zc_ref, pc_ref):
    g = pl.program_id(0)
    i = pl.program_id(1)
    tm = x_ref.shape[0]

    @pl.when(i == 0)
    def _():
        zc_ref[...] = jnp.zeros_like(zc_ref)
        pc_ref[...] = jnp.zeros_like(pc_ref)

    x = x_ref[...]
    proj = lambda w_ref: jnp.dot(x, w_ref[...], preferred_element_type=F32)

    u = _gelu(proj(wu_ref))
    v = _gelu(proj(wv_ref))
    mu = jnp.mean(v, axis=-1, keepdims=True)
    vc = v - mu
    var = jnp.mean(vc * vc, axis=-1, keepdims=True)
    vn = (vc * lax.rsqrt(var + LN_EPS)).astype(BF16)
    qi = lax.broadcasted_iota(I32, (A_BLOCK, A_BLOCK), 0) // CHUNK
    kj = lax.broadcasted_iota(I32, (A_BLOCK, A_BLOCK), 1) // CHUNK
    ws = jnp.where(kj <= qi, ws_ref[0], 0.0).astype(BF16)
    bs = bs_ref[0]
    for n in range(tm // A_BLOCK):
        rows = slice(n * A_BLOCK, (n + 1) * A_BLOCK)
        mixed = jnp.dot(ws, vn[rows], preferred_element_type=F32) + bs
        ya_ref[rows, :] = (u[rows] * mixed).astype(BF16)

    z = proj(wcg_ref) * proj(wh_ref)
    row = lax.broadcasted_iota(I32, (tm, 1), 0)
    prev1 = zc_ref[7:8, :]
    prev2 = zc_ref[6:7, :]
    z1 = jnp.where(row == 0, prev1, pltpu.roll(z, 1, 0))
    z2 = jnp.where(row == 0, prev2, jnp.where(row == 1, prev1, pltpu.roll(z, 2, 0)))
    conv = wc_ref[0:1, :] * z2 + wc_ref[1:2, :] * z1 + wc_ref[2:3, :] * z
    yb_ref[...] = (proj(wbg_ref) * conv).astype(BF16)
    zc_ref[...] = z[tm - 8:, :]

    p = proj(wpp_ref)
    win = lax.shift_left(jnp.int32(2), g)
    p16 = p.astype(BF16)
    pext = jnp.concatenate([pc_ref[...], p16], axis=0)
    rr = lax.broadcasted_iota(I32, (tm, tm + A_BLOCK), 0) + A_BLOCK
    cc = lax.broadcasted_iota(I32, (tm, tm + A_BLOCK), 1)
    band = jnp.where((cc <= rr) & (cc > rr - win), 1.0, 0.0).astype(BF16)
    total = jnp.dot(band, pext, preferred_element_type=F32)
    count = jnp.minimum(i * tm + row + 1, win).astype(F32)
    pooled = (total / count - p).astype(BF16)
    yc = jnp.dot(pooled, wp_ref[0], preferred_element_type=F32) * ps_ref[...]
    yc_ref[...] = yc.astype(BF16)
    pc_ref[...] = p16[tm - A_BLOCK:, :]


def _mix_tokens_call(xb, w_in16, w_s, b_s, w_conv, w_pool16, pool_scale, tm=512):
    t, d = xb.shape
    stream = lambda s: pl.BlockSpec((d, HEAD), lambda g, i: (0, s * N_HEAD_GROUPS + g))
    grp = lambda *shape: pl.BlockSpec((1,) + shape, lambda g, i: (g,) + (0,) * len(shape))
    out_spec = pl.BlockSpec((tm, HEAD), lambda g, i: (i, g))
    out_shape = jax.ShapeDtypeStruct((t, A_WIDTH), BF16)
    return pl.pallas_call(
        _mix_tokens_kernel,
        grid=(N_HEAD_GROUPS, t // tm),
        in_specs=[pl.BlockSpec((tm, d), lambda g, i: (i, 0))]
        + [stream(s) for s in range(N_STREAMS)]
        + [grp(A_BLOCK, A_BLOCK), grp(A_BLOCK, 1),
           pl.BlockSpec((CONV_W, HEAD), lambda g, i: (0, g)),
           grp(HEAD, HEAD),
           pl.BlockSpec((1, HEAD), lambda g, i: (0, g))],
        out_specs=[out_spec, out_spec, out_spec],
        out_shape=[out_shape, out_shape, out_shape],
        scratch_shapes=[pltpu.VMEM((8, HEAD), F32), pltpu.VMEM((A_BLOCK, HEAD), BF16)],
        compiler_params=_params(("arbitrary", "arbitrary")),
        name="mix_tokens",
    )(xb, *([w_in16] * N_STREAMS), w_s, b_s.reshape(N_HEAD_GROUPS, A_BLOCK, 1), w_conv,
      w_pool16, pool_scale.reshape(1, A_WIDTH))


def _merge_kernel(x_ref, ya_ref, yb_ref, yc_ref, wg0_ref, wg1_ref, wg2_ref, bg0_ref, bg1_ref, bg2_ref,
                  wa_ref, wb_ref, wc_ref, m_ref):
    x = x_ref[...]
    gate = lambda w_ref, b_ref: jax.nn.sigmoid(
        jnp.dot(x, w_ref[...], preferred_element_type=F32) + b_ref[...])
    m = gate(wg0_ref, bg0_ref) * jnp.dot(ya_ref[...], wa_ref[...], preferred_element_type=F32)
    m = m + gate(wg1_ref, bg1_ref) * jnp.dot(yb_ref[...], wb_ref[...], preferred_element_type=F32)
    m = m + gate(wg2_ref, bg2_ref) * jnp.dot(yc_ref[...], wc_ref[...], preferred_element_type=F32)
    m_ref[...] = m.astype(BF16)


def _merge_call(xb, ya, yb, yc, w_gate16, b_gate, wa16, wb16, wc16, tm=512):
    t, d = xb.shape
    nj = d // MERGE_COLS
    tok = lambda width: pl.BlockSpec((tm, width), lambda j, i: (i, 0))
    gate_w = lambda br: pl.BlockSpec((d, MERGE_COLS), lambda j, i: (0, br * nj + j))
    gate_b = lambda br: pl.BlockSpec((1, MERGE_COLS), lambda j, i: (0, br * nj + j))
    branch_w = pl.BlockSpec((A_WIDTH, MERGE_COLS), lambda j, i: (0, j))
    b2 = b_gate.reshape(1, N_BRANCH * d)
    return pl.pallas_call(
        _merge_kernel,
        grid=(nj, t // tm),
        in_specs=[tok(d), tok(A_WIDTH), tok(A_WIDTH), tok(A_WIDTH)]
        + [gate_w(br) for br in range(N_BRANCH)] + [gate_b(br) for br in range(N_BRANCH)]
        + [branch_w] * N_BRANCH,
        out_specs=pl.BlockSpec((tm, MERGE_COLS), lambda j, i: (i, j)),
        out_shape=jax.ShapeDtypeStruct((t, d), BF16),
        compiler_params=_params(("arbitrary", "arbitrary")),
        name="merge_branches",
    )(xb, ya, yb, yc, w_gate16, w_gate16, w_gate16, b2, b2, b2, wa16, wb16, wc16)


def _out_proj_kernel(m_ref, x_ref, w_ref, g_ref, b_ref, o32_ref, o16_ref):
    r = ALPHA * x_ref[...] + jnp.dot(m_ref[...], w_ref[...], preferred_element_type=F32)
    y = _layer_norm_rows(r, g_ref[...], b_ref[...])
    o32_ref[...] = y
    o16_ref[...] = y.astype(BF16)


def _out_proj_call(merged, x, w_out16, g, b, tm=512):
    t, d = x.shape
    row = pl.BlockSpec((tm, d), lambda i: (i, 0))
    vec = pl.BlockSpec((1, d), lambda i: (0, 0))
    return pl.pallas_call(
        _out_proj_kernel,
        grid=(t // tm,),
        in_specs=[row, row, pl.BlockSpec((d, d), lambda i: (0, 0)), vec, vec],
        out_specs=[row, row],
        out_shape=[jax.ShapeDtypeStruct((t, d), F32), jax.ShapeDtypeStruct((t, d), BF16)],
        compiler_params=_params(("parallel",)),
        name="out_proj_ln",
    )(merged, x, w_out16, g.reshape(1, d), b.reshape(1, d))


def _dot_nt(a, b):
    return lax.dot_general(a, b, (((1,), (1,)), ((), ())), preferred_element_type=F32)


def _split_bf16(v):
    hi = v.astype(BF16)
    return hi, (v - hi.astype(F32)).astype(BF16)


def _to_groups(v):
    return jnp.stack([v[g * GROUP_SIZE:(g + 1) * GROUP_SIZE] for g in range(N_GROUPS)])


def _from_groups(v):
    return jnp.concatenate([v[g] for g in range(N_GROUPS)], axis=0)


def _over_experts(fn, v):
    return fn(fn(v, axis=0, keepdims=True), axis=1, keepdims=True)


def _router_kernel(x_ref, wt_ref, b_ref, wts_ref, pos_ref, cnt_ref, start_ref):
    tm = x_ref.shape[0]
    neg = -jnp.inf
    xh, xl = _split_bf16(x_ref[...])
    wh, wl = _split_bf16(wt_ref[...])
    logits = _dot_nt(wh, xh) + (_dot_nt(wh, xl) + _dot_nt(wl, xh))
    scores2d = jax.nn.sigmoid(logits)
    scores = _to_groups(scores2d)
    sel = _to_groups(scores2d + b_ref[...])
    shape3 = (N_GROUPS, GROUP_SIZE, tm)
    member = lax.broadcasted_iota(I32, shape3, 1)
    expert = lax.broadcasted_iota(I32, shape3, 0) * GROUP_SIZE + member

    m1 = jnp.max(sel, axis=1, keepdims=True)
    i1 = jnp.min(jnp.where(sel == m1, member, GROUP_SIZE), axis=1, keepdims=True)
    m2 = jnp.max(jnp.where(member == i1, neg, sel), axis=1, keepdims=True)
    gscore = m1 + m2

    gid = lax.broadcasted_iota(I32, (N_GROUPS, 1, tm), 0)
    keep = jnp.zeros((N_GROUPS, 1, tm), jnp.bool_)
    for _ in range(TOPK_GROUPS):
        m = jnp.max(gscore, axis=0, keepdims=True)
        gsel = jnp.min(jnp.where(gscore == m, gid, N_GROUPS), axis=0, keepdims=True)
        hit = gid == gsel
        keep = keep | hit
        gscore = jnp.where(hit, neg, gscore)

    cand = jnp.where(keep, sel, neg)
    chosen, weight = [], []
    for _ in range(TOP_K):
        m = _over_experts(jnp.max, cand)
        idx = _over_experts(jnp.min, jnp.where(cand == m, expert, N_EXPERTS))
        hit = expert == idx
        chosen.append(idx)
        weight.append(_over_experts(jnp.sum, jnp.where(hit, scores, 0.0))[0])
        cand = jnp.where(hit, neg, cand)
    norm = functools.reduce(jnp.add, weight)
    wts_ref[0] = jnp.concatenate(weight, axis=0) / norm * ROUTED_SCALE

    onehot = functools.reduce(jnp.add, [jnp.where(expert == idx, 1.0, 0.0) for idx in chosen])
    onehot = _from_groups(onehot)
    t_r = lax.broadcasted_iota(I32, (tm, tm), 0)
    t_c = lax.broadcasted_iota(I32, (tm, tm), 1)
    earlier = jnp.where(t_r < t_c, 1.0, 0.0).astype(BF16)
    rank = jnp.dot(onehot.astype(BF16), earlier, preferred_element_type=F32)
    count = jnp.sum(onehot, axis=1, keepdims=True)
    e_r = lax.broadcasted_iota(I32, (N_EXPERTS, N_EXPERTS), 0)
    e_c = lax.broadcasted_iota(I32, (N_EXPERTS, N_EXPERTS), 1)
    before = jnp.where(e_c < e_r, 1.0, 0.0).astype(BF16)
    start = jnp.dot(before, jnp.broadcast_to(count, (N_EXPERTS, tm)).astype(BF16),
                    preferred_element_type=F32)
    where_to = _to_groups(rank + start)
    pos = [_over_experts(jnp.sum, jnp.where(expert == idx, where_to, 0.0))[0] for idx in chosen]
    pos_ref[0] = jnp.concatenate(pos, axis=0).astype(I32)
    cnt_ref[0] = count.astype(I32)
    start_ref[0] = start[:, 0:1].astype(I32)


def _router_call(x, w_router, router_bias):
    t, d = x.shape
    nw = t // WINDOW
    per_tok = pl.BlockSpec((1, TOP_K, WINDOW), lambda i: (i, 0, 0))
    per_win = pl.BlockSpec((1, N_EXPERTS, 1), lambda i: (i, 0, 0))
    return pl.pallas_call(
        _router_kernel,
        grid=(nw,),
        in_specs=[pl.BlockSpec((WINDOW, d), lambda i: (i, 0)),
                  pl.BlockSpec((N_EXPERTS, d), lambda i: (0, 0)),
                  pl.BlockSpec((N_EXPERTS, 1), lambda i: (0, 0))],
        out_specs=[per_tok, per_tok, per_win, per_win],
        out_shape=[jax.ShapeDtypeStruct((nw, TOP_K, WINDOW), F32),
                   jax.ShapeDtypeStruct((nw, TOP_K, WINDOW), I32),
                   jax.ShapeDtypeStruct((nw, N_EXPERTS, 1), I32),
                   jax.ShapeDtypeStruct((nw, N_EXPERTS, 1), I32)],
        compiler_params=_params(("parallel",)),
        name="router",
    )(x, w_router.T, router_bias.reshape(N_EXPERTS, 1))


def _shared_kernel(x32_ref, x16_ref, wg_ref, wu_ref, wd_ref, o_ref):
    xb = x16_ref[...]
    a = jnp.dot(xb, wg_ref[...], preferred_element_type=F32)
    b = jnp.dot(xb, wu_ref[...], preferred_element_type=F32)
    hid = (jax.nn.silu(a) * b).astype(BF16)
    o_ref[...] = ALPHA * x32_ref[...] + jnp.dot(hid, wd_ref[...], preferred_element_type=F32)


def _shared_call(x32, x16, wg, wu, wd, tm=512):
    t, d = x32.shape
    hdim = wg.shape[1]
    row = pl.BlockSpec((tm, d), lambda i: (i, 0))
    return pl.pallas_call(
        _shared_kernel,
        grid=(t // tm,),
        in_specs=[row, row,
                  pl.BlockSpec((d, hdim), lambda i: (0, 0)),
                  pl.BlockSpec((d, hdim), lambda i: (0, 0)),
                  pl.BlockSpec((hdim, d), lambda i: (0, 0))],
        out_specs=row,
        out_shape=jax.ShapeDtypeStruct((t, d), F32),
        compiler_params=_params(("parallel",)),
        name="shared_expert",
    )(x32, x16, wg, wu, wd)


def _segment_copies(cnt_ref, start_ref, off_ref, win, local, remote, sem, to_remote, act):
    def body(e, carry):
        n = cnt_ref[win * N_EXPERTS + e]

        @pl.when(n > 0)
        def _():
            loc = local.at[pl.ds(start_ref[win * N_EXPERTS + e], n)]
            rem = remote.at[pl.ds(off_ref[win * N_EXPERTS + e], n)]
            act(pltpu.make_async_copy(loc, rem, sem) if to_remote
                else pltpu.make_async_copy(rem, loc, sem))
        return carry
    lax.fori_loop(0, N_EXPERTS, body, 0)


def _pad_copies(pad_start_ref, pad_len_ref, zeros, remote, sem, act):
    def body(e, carry):
        n = pad_len_ref[e]

        @pl.when(n > 0)
        def _():
            act(pltpu.make_async_copy(zeros.at[pl.ds(0, n)], remote.at[pl.ds(pad_start_ref[e], n)], sem))
        return carry
    lax.fori_loop(0, N_EXPERTS, body, 0)


def _dispatch_kernel(cnt_ref, start_ref, off_ref, pad_start_ref, pad_len_ref,
                     pos_ref, x_ref, rows_hbm, buf, zeros, sem, pad_sem):
    w = pl.program_id(0)
    last = pl.num_programs(0) - 1
    slot = w % 2
    start = lambda c: c.start()
    wait = lambda c: c.wait()
    copies = functools.partial(_segment_copies, cnt_ref, start_ref, off_ref,
                               remote=rows_hbm, to_remote=True)

    @pl.when(w == 0)
    def _():
        zeros[...] = jnp.zeros_like(zeros)
        _pad_copies(pad_start_ref, pad_len_ref, zeros, rows_hbm, pad_sem, start)

    @pl.when(w >= 2)
    def _():
        copies(win=w - 2, local=buf.at[slot], sem=sem.at[slot], act=wait)

    def place(t, carry):
        v = pltpu.bitcast(x_ref[t], U32)
        for k in range(TOP_K):
            buf[slot, pos_ref[0, 0, k * WINDOW + t]] = v
        return carry
    lax.fori_loop(0, WINDOW, place, 0)
    copies(win=w, local=buf.at[slot], sem=sem.at[slot], act=start)

    @pl.when(w == last)
    def _():
        @pl.when(w >= 1)
        def _():
            copies(win=w - 1, local=buf.at[1 - slot], sem=sem.at[1 - slot], act=wait)
        copies(win=w, local=buf.at[slot], sem=sem.at[slot], act=wait)
        _pad_copies(pad_start_ref, pad_len_ref, zeros, rows_hbm, pad_sem, wait)


def _dispatch_call(x_tiles, pos, cnt, start, off, pad_start, pad_len, n_rows):
    t = x_tiles.shape[0]
    nw = t // WINDOW
    grid_spec = pltpu.PrefetchScalarGridSpec(
        num_scalar_prefetch=5,
        grid=(nw,),
        in_specs=[pl.BlockSpec((1, 1, PAIRS), lambda w, *_: (w, 0, 0), memory_space=pltpu.SMEM),
                  pl.BlockSpec((WINDOW, ROW_TILES, LANES), lambda w, *_: (w, 0, 0))],
        out_specs=pl.BlockSpec(memory_space=pl.ANY),
        scratch_shapes=[pltpu.VMEM((2, PAIRS, WORD_ROWS, LANES), U32),
                        pltpu.VMEM((EXPERT_TILE, WORD_ROWS, LANES), U32),
                        pltpu.SemaphoreType.DMA((2,)),
                        pltpu.SemaphoreType.DMA(())],
    )
    return pl.pallas_call(
        _dispatch_kernel,
        grid_spec=grid_spec,
        out_shape=jax.ShapeDtypeStruct((n_rows, WORD_ROWS, LANES), U32),
        compiler_params=_params(("arbitrary",)),
        name="dispatch",
    )(cnt, start, off, pad_start, pad_len, pos, x_tiles)


def _expert_kernel(tile_expert_ref, tile_block_ref, tile_first_ref, tile_valid_ref,
                   rows_ref, w1_ref, w3_ref, w2_ref, out_ref, w1b, w3b, w2b, xt):
    i = pl.program_id(0)
    tr = xt.shape[0]

    @pl.when(tile_valid_ref[i] == 1)
    def _():
        @pl.when(tile_first_ref[i] == 1)
        def _():
            w1b[...] = w1_ref[0, 0].astype(BF16)
            w3b[...] = w3_ref[0, 0].astype(BF16)
            w2b[...] = w2_ref[0, 0].astype(BF16)

        for s in range(WORD_ROWS):
            low, high = _unpack_words(rows_ref[pl.ds(s, tr, stride=WORD_ROWS), :])
            xt[:, (2 * s) * LANES:(2 * s + 1) * LANES] = low.astype(BF16)
            xt[:, (2 * s + 1) * LANES:(2 * s + 2) * LANES] = high.astype(BF16)
        x = xt[...]
        a = jnp.dot(x, w1b[...], preferred_element_type=F32)
        b = jnp.dot(x, w3b[...], preferred_element_type=F32)
        hid = (jax.nn.silu(a) * b).astype(BF16)
        y = jnp.dot(hid, w2b[...], preferred_element_type=F32)
        for s in range(WORD_ROWS):
            out_ref[pl.ds(s, tr, stride=WORD_ROWS), :] = _pack_words(
                y[:, (2 * s) * LANES:(2 * s + 1) * LANES], y[:, (2 * s + 1) * LANES:(2 * s + 2) * LANES])


def _expert_call(rows2d, tile_expert, tile_block, tile_first, tile_valid, layer, w1, w3, w2):
    n_tiles = tile_expert.shape[0]
    _, _, d, hdim = w1.shape
    blk = EXPERT_TILE * WORD_ROWS
    row_spec = pl.BlockSpec((blk, LANES), lambda i, te, tb, tf, tv: (tb[i], 0))
    grid_spec = pltpu.PrefetchScalarGridSpec(
        num_scalar_prefetch=4,
        grid=(n_tiles,),
        in_specs=[row_spec,
                  pl.BlockSpec((1, 1, d, hdim), lambda i, te, tb, tf, tv: (layer, te[i], 0, 0)),
                  pl.BlockSpec((1, 1, d, hdim), lambda i, te, tb, tf, tv: (layer, te[i], 0, 0)),
                  pl.BlockSpec((1, 1, hdim, d), lambda i, te, tb, tf, tv: (layer, te[i], 0, 0))],
        out_specs=row_spec,
        scratch_shapes=[pltpu.VMEM((d, hdim), BF16), pltpu.VMEM((d, hdim), BF16),
                        pltpu.VMEM((hdim, d), BF16), pltpu.VMEM((EXPERT_TILE, d), BF16)],
    )
    return pl.pallas_call(
        _expert_kernel,
        grid_spec=grid_spec,
        out_shape=jax.ShapeDtypeStruct(rows2d.shape, U32),
        compiler_params=_params(("arbitrary",)),
        name="routed_experts",
    )(tile_expert, tile_block, tile_first, tile_valid, rows2d, w1, w3, w2)


def _combine_kernel(cnt_ref, start_ref, off_ref, pos_ref, wts_ref, rows_hbm, s_ref, g_ref, b_ref,
                    o32_ref, o16_ref, buf, acc_lo, acc_hi, sem):
    w = pl.program_id(0)
    last = pl.num_programs(0) - 1
    slot = w % 2
    start = lambda c: c.start()
    wait = lambda c: c.wait()
    copies = functools.partial(_segment_copies, cnt_ref, start_ref, off_ref,
                               remote=rows_hbm, to_remote=False)

    @pl.when(w == 0)
    def _():
        copies(win=w, local=buf.at[slot], sem=sem.at[slot], act=start)

    @pl.when(w < last)
    def _():
        copies(win=w + 1, local=buf.at[1 - slot], sem=sem.at[1 - slot], act=start)

    copies(win=w, local=buf.at[slot], sem=sem.at[slot], act=wait)

    def token(t, carry):
        lo, hi = _unpack_words(buf[slot, pos_ref[0, 0, t]])
        wt = wts_ref[0, 0, t]
        lo, hi = wt * lo, wt * hi
        for k in range(1, TOP_K):
            l_k, h_k = _unpack_words(buf[slot, pos_ref[0, 0, k * WINDOW + t]])
            wt = wts_ref[0, 0, k * WINDOW + t]
            lo, hi = lo + wt * l_k, hi + wt * h_k
        at = pl.ds(pl.multiple_of(t * WORD_ROWS, WORD_ROWS), WORD_ROWS)
        acc_lo[at, :] = lo
        acc_hi[at, :] = hi
        return carry
    lax.fori_loop(0, WINDOW, token, 0)

    chunks = []
    for s in range(WORD_ROWS):
        chunks.append(acc_lo[pl.ds(s, WINDOW, stride=WORD_ROWS), :])
        chunks.append(acc_hi[pl.ds(s, WINDOW, stride=WORD_ROWS), :])
    out = _layer_norm_rows(s_ref[...] + jnp.concatenate(chunks, axis=1), g_ref[...], b_ref[...])
    o32_ref[...] = out
    o16_ref[...] = out.astype(BF16)


def _combine_call(rows_tiles, pos, wts, cnt, start, off, s, g, b):
    t, d = s.shape
    nw = t // WINDOW
    smem_block = pl.BlockSpec((1, 1, PAIRS), lambda w, *_: (w, 0, 0), memory_space=pltpu.SMEM)
    row = pl.BlockSpec((WINDOW, d), lambda w, *_: (w, 0))
    vec = pl.BlockSpec((1, d), lambda w, *_: (0, 0))
    grid_spec = pltpu.PrefetchScalarGridSpec(
        num_scalar_prefetch=3,
        grid=(nw,),
        in_specs=[smem_block, smem_block, pl.BlockSpec(memory_space=pl.ANY), row, vec, vec],
        out_specs=[row, row],
        scratch_shapes=[pltpu.VMEM((2, PAIRS, WORD_ROWS, LANES), U32),
                        pltpu.VMEM((WINDOW * WORD_ROWS, LANES), F32),
                        pltpu.VMEM((WINDOW * WORD_ROWS, LANES), F32),
                        pltpu.SemaphoreType.DMA((2,))],
    )
    return pl.pallas_call(
        _combine_kernel,
        grid_spec=grid_spec,
        out_shape=[jax.ShapeDtypeStruct((t, d), F32), jax.ShapeDtypeStruct((t, d), BF16)],
        compiler_params=_params(("arbitrary",)),
        name="combine_ln",
    )(cnt, start, off, pos, wts, rows_tiles, s, g.reshape(1, d), b.reshape(1, d))


def _row_layout(cnt, n_tiles):
    cnt2 = cnt.reshape(-1, N_EXPERTS)
    total = jnp.sum(cnt2, axis=0)
    padded = (total + EXPERT_TILE - 1) // EXPERT_TILE * EXPERT_TILE
    ends = jnp.cumsum(padded)
    base = ends - padded
    off = base[None, :] + jnp.cumsum(cnt2, axis=0) - cnt2
    used = ends[-1] // EXPERT_TILE
    tile = jnp.arange(n_tiles, dtype=I32)
    tile_block = jnp.minimum(tile, used - 1)
    tile_expert = jnp.sum((ends[None, :] <= (tile_block * EXPERT_TILE)[:, None]).astype(I32), axis=1)
    tile_expert = jnp.minimum(tile_expert, N_EXPERTS - 1)
    tile_valid = (tile < used).astype(I32)
    tile_first = tile_valid * (tile * EXPERT_TILE == base[tile_expert]).astype(I32)
    return (off.reshape(-1).astype(I32), (base + total).astype(I32), (padded - total).astype(I32),
            tile_expert.astype(I32), tile_block.astype(I32), tile_first, tile_valid)


def _moe_block(x32, x16, layer, w_router, router_bias, w1, w3, w2, ws1, ws3, ws2, g, b):
    t, d = x32.shape
    nw = t // WINDOW
    n_tiles = t * TOP_K // EXPERT_TILE + N_EXPERTS
    n_rows = n_tiles * EXPERT_TILE
    wts, pos, cnt, start = _router_call(x32, w_router, router_bias)
    off, pad_start, pad_len, tile_expert, tile_block, tile_first, tile_valid = _row_layout(cnt, n_tiles)
    cnt = cnt.reshape(-1)
    start = start.reshape(-1)
    pos = pos.reshape(nw, 1, PAIRS)
    wts = wts.reshape(nw, 1, PAIRS)
    rows = _dispatch_call(x16.reshape(t, ROW_TILES, LANES), pos, cnt, start, off, pad_start, pad_len, n_rows)
    s = _shared_call(x32, x16, ws1.astype(BF16), ws3.astype(BF16), ws2.astype(BF16))
    out_rows = _expert_call(rows.reshape(n_rows * WORD_ROWS, LANES),
                            tile_expert, tile_block, tile_first, tile_valid, layer, w1, w3, w2)
    return _combine_call(out_rows.reshape(n_rows, WORD_ROWS, LANES), pos, wts, cnt, start, off, s, g, b)


def kernel(x, in_ln_g, in_ln_b, w_in, w_spatial, b_spatial, w_conv, w_pool, pool_scale,
           w_gate, b_gate, w_proj_a, w_proj_b, w_proj_c, w_out, ln1_g, ln1_b,
           w_router, router_bias, w_expert_gate, w_expert_up, w_expert_down,
           w_shared_gate, w_shared_up, w_shared_down, ln2_g, ln2_b):
    bsz, seq, d = x.shape
    x32, x16 = _ln_call(x.reshape(bsz * seq, d), in_ln_g, in_ln_b)
    for l in range(DEPTH):
        ya, yb, yc = _mix_tokens_call(x16, w_in[l].astype(BF16), w_spatial[l], b_spatial[l], w_conv[l],
                                      w_pool[l].astype(BF16), pool_scale[l])
        merged = _merge_call(x16, ya, yb, yc, w_gate[l].astype(BF16), b_gate[l],
                             w_proj_a[l].astype(BF16), w_proj_b[l].astype(BF16), w_proj_c[l].astype(BF16))
        x32, x16 = _out_proj_call(merged, x32, w_out[l].astype(BF16), ln1_g[l], ln1_b[l])
        x32, x16 = _moe_block(x32, x16, l, w_router[l], router_bias[l], w_expert_gate, w_expert_up,
                              w_expert_down, w_shared_gate[l], w_shared_up[l], w_shared_down[l],
                              ln2_g[l], ln2_b[l])
    return x32.reshape(bsz, seq, d)
```

```python
import functools

import jax
import jax.numpy as jnp
from jax import lax
from jax.experimental import pallas as pl
from jax.experimental.pallas import tpu as pltpu

F32 = jnp.float32
BF16 = jnp.bfloat16
I32 = jnp.int32
U32 = jnp.uint32

D_MODEL = 2048
DEPTH = 2
CHUNK = 64
A_WIDTH = 1024
A_BLOCK = 128
N_HEAD_GROUPS = 4
HEAD = A_WIDTH // N_HEAD_GROUPS
N_STREAMS = 6
CONV_W = 3
N_BRANCH = 3
N_EXPERTS = 64
TOP_K = 8
N_GROUPS = 8
GROUP_SIZE = N_EXPERTS // N_GROUPS
TOPK_GROUPS = 4
ROUTED_SCALE = 2.5
ALPHA = (2.0 * DEPTH) ** 0.25
LN_EPS = 1e-5

LANES = 128
ROW_TILES = D_MODEL // LANES
WORD_ROWS = ROW_TILES // 2
MERGE_COLS = 512
WINDOW = 128
PAIRS = WINDOW * TOP_K
EXPERT_TILE = 256
VMEM_LIMIT = 56 * 1024 * 1024
HIGH_HALF = 0xFFFF0000


def _params(sem, vmem=VMEM_LIMIT):
    return pltpu.CompilerParams(dimension_semantics=sem, vmem_limit_bytes=vmem)


def _gelu(x):
    return 0.5 * x * (1.0 + lax.erf(x * (0.5 ** 0.5)))


def _layer_norm_rows(r, g, b):
    mu = jnp.mean(r, axis=-1, keepdims=True)
    c = r - mu
    var = jnp.mean(c * c, axis=-1, keepdims=True)
    return c * lax.rsqrt(var + LN_EPS) * g + b


def _unpack_words(words):
    low = pltpu.bitcast(lax.shift_left(words, jnp.uint32(16)), F32)
    high = pltpu.bitcast(words & jnp.uint32(HIGH_HALF), F32)
    return low, high


def _pack_words(low, high):
    lo = pltpu.bitcast(low.astype(BF16).astype(F32), U32)
    hi = pltpu.bitcast(high.astype(BF16).astype(F32), U32)
    return lax.shift_right_logical(lo, jnp.uint32(16)) | (hi & jnp.uint32(HIGH_HALF))


def _ln_kernel(x_ref, g_ref, b_ref, o32_ref, o16_ref):
    y = _layer_norm_rows(x_ref[...], g_ref[...], b_ref[...])
    o32_ref[...] = y
    o16_ref[...] = y.astype(BF16)


def _ln_call(x, g, b, tm=256):
    t, d = x.shape
    row = pl.BlockSpec((tm, d), lambda i: (i, 0))
    vec = pl.BlockSpec((1, d), lambda i: (0, 0))
    return pl.pallas_call(
        _ln_kernel,
        grid=(t // tm,),
        in_specs=[row, vec, vec],
        out_specs=[row, row],
        out_shape=[jax.ShapeDtypeStruct((t, d), F32), jax.ShapeDtypeStruct((t, d), BF16)],
        compiler_params=_params(("parallel",)),
        name="layer_norm",
    )(x, g.reshape(1, d), b.reshape(1, d))


def _mix_tokens_kernel(x_ref, wu_ref, wv_ref, wbg_ref, wcg_ref, wh_ref, wpp_ref,
                       ws_ref, bs_ref, wc_ref, wp_ref, ps_ref,
                       ya_ref, yb_ref, yc_ref, zc_ref, pc_ref):
    g = pl.program_id(0)
    i = pl.program_id(1)
    tm = x_ref.shape[0]

    @pl.when(i == 0)
    def _():
        zc_ref[...] = jnp.zeros_like(zc_ref)
        pc_ref[...] = jnp.zeros_like(pc_ref)

    x = x_ref[...]
    proj = lambda w_ref: jnp.dot(x, w_ref[...], preferred_element_type=F32)

    u = _gelu(proj(wu_ref))
    v = _gelu(proj(wv_ref))
    mu = jnp.mean(v, axis=-1, keepdims=True)
    vc = v - mu
    var = jnp.mean(vc * vc, axis=-1, keepdims=True)
    vn = (vc * lax.rsqrt(var + LN_EPS)).astype(BF16)
    qi = lax.broadcasted_iota(I32, (A_BLOCK, A_BLOCK), 0) // CHUNK
    kj = lax.broadcasted_iota(I32, (A_BLOCK, A_BLOCK), 1) // CHUNK
    ws = jnp.where(kj <= qi, ws_ref[0], 0.0).astype(BF16)
    bs = bs_ref[0]
    for n in range(tm // A_BLOCK):
        rows = slice(n * A_BLOCK, (n + 1) * A_BLOCK)
        mixed = jnp.dot(ws, vn[rows], preferred_element_type=F32) + bs
        ya_ref[rows, :] = (u[rows] * mixed).astype(BF16)

    z = proj(wcg_ref) * proj(wh_ref)
    row = lax.broadcasted_iota(I32, (tm, 1), 0)
    prev1 = zc_ref[7:8, :]
    prev2 = zc_ref[6:7, :]
    z1 = jnp.where(row == 0, prev1, pltpu.roll(z, 1, 0))
    z2 = jnp.where(row == 0, prev2, jnp.where(row == 1, prev1, pltpu.roll(z, 2, 0)))
    conv = wc_ref[0:1, :] * z2 + wc_ref[1:2, :] * z1 + wc_ref[2:3, :] * z
    yb_ref[...] = (proj(wbg_ref) * conv).astype(BF16)
    zc_ref[...] = z[tm - 8:, :]

    p = proj(wpp_ref)
    win = lax.shift_left(jnp.int32(2), g)
    p16 = p.astype(BF16)
    pext = jnp.concatenate([pc_ref[...], p16], axis=0)
    rr = lax.broadcasted_iota(I32, (tm, tm + A_BLOCK), 0) + A_BLOCK
    cc = lax.broadcasted_iota(I32, (tm, tm + A_BLOCK), 1)
    band = jnp.where((cc <= rr) & (cc > rr - win), 1.0, 0.0).astype(BF16)
    total = jnp.dot(band, pext, preferred_element_type=F32)
    count = jnp.minimum(i * tm + row + 1, win).astype(F32)
    pooled = (total / count - p).astype(BF16)
    yc = jnp.dot(pooled, wp_ref[0], preferred_element_type=F32) * ps_ref[...]
    yc_ref[...] = yc.astype(BF16)
    pc_ref[...] = p16[tm - A_BLOCK:, :]


def _mix_tokens_call(xb, w_in16, w_s, b_s, w_conv, w_pool16, pool_scale, tm=512):
    t, d = xb.shape
    stream = lambda s: pl.BlockSpec((d, HEAD), lambda g, i: (0, s * N_HEAD_GROUPS + g))
    grp = lambda *shape: pl.BlockSpec((1,) + shape, lambda g, i: (g,) + (0,) * len(shape))
    out_spec = pl.BlockSpec((tm, HEAD), lambda g, i: (i, g))
    out_shape = jax.ShapeDtypeStruct((t, A_WIDTH), BF16)
    return pl.pallas_call(
        _mix_tokens_kernel,
        grid=(N_HEAD_GROUPS, t // tm),
        in_specs=[pl.BlockSpec((tm, d), lambda g, i: (i, 0))]
        + [stream(s) for s in range(N_STREAMS)]
        + [grp(A_BLOCK, A_BLOCK), grp(A_BLOCK, 1),
           pl.BlockSpec((CONV_W, HEAD), lambda g, i: (0, g)),
           grp(HEAD, HEAD),
           pl.BlockSpec((1, HEAD), lambda g, i: (0, g))],
        out_specs=[out_spec, out_spec, out_spec],
        out_shape=[out_shape, out_shape, out_shape],
        scratch_shapes=[pltpu.VMEM((8, HEAD), F32), pltpu.VMEM((A_BLOCK, HEAD), BF16)],
        compiler_params=_params(("arbitrary", "arbitrary")),
        name="mix_tokens",
    )(xb, *([w_in16] * N_STREAMS), w_s, b_s.reshape(N_HEAD_GROUPS, A_BLOCK, 1), w_conv,
      w_pool16, pool_scale.reshape(1, A_WIDTH))


def _merge_kernel(x_ref, ya_ref, yb_ref, yc_ref, wg0_ref, wg1_ref, wg2_ref, bg0_ref, bg1_ref, bg2_ref,
                  wa_ref, wb_ref, wc_ref, m_ref):
    x = x_ref[...]
    gate = lambda w_ref, b_ref: jax.nn.sigmoid(
        jnp.dot(x, w_ref[...], preferred_element_type=F32) + b_ref[...])
    m = gate(wg0_ref, bg0_ref) * jnp.dot(ya_ref[...], wa_ref[...], preferred_element_type=F32)
    m = m + gate(wg1_ref, bg1_ref) * jnp.dot(yb_ref[...], wb_ref[...], preferred_element_type=F32)
    m = m + gate(wg2_ref, bg2_ref) * jnp.dot(yc_ref[...], wc_ref[...], preferred_element_type=F32)
    m_ref[...] = m.astype(BF16)


def _merge_call(xb, ya, yb, yc, w_gate16, b_gate, wa16, wb16, wc16, tm=512):
    t, d = xb.shape
    nj = d // MERGE_COLS
    tok = lambda width: pl.BlockSpec((tm, width), lambda j, i: (i, 0))
    gate_w = lambda br: pl.BlockSpec((d, MERGE_COLS), lambda j, i: (0, br * nj + j))
    gate_b = lambda br: pl.BlockSpec((1, MERGE_COLS), lambda j, i: (0, br * nj + j))
    branch_w = pl.BlockSpec((A_WIDTH, MERGE_COLS), lambda j, i: (0, j))
    b2 = b_gate.reshape(1, N_BRANCH * d)
    return pl.pallas_call(
        _merge_kernel,
        grid=(nj, t // tm),
        in_specs=[tok(d), tok(A_WIDTH), tok(A_WIDTH), tok(A_WIDTH)]
        + [gate_w(br) for br in range(N_BRANCH)] + [gate_b(br) for br in range(N_BRANCH)]
        + [branch_w] * N_BRANCH,
        out_specs=pl.BlockSpec((tm, MERGE_COLS), lambda j, i: (i, j)),
        out_shape=jax.ShapeDtypeStruct((t, d), BF16),
        compiler_params=_params(("arbitrary", "arbitrary")),
        name="merge_branches",
    )(xb, ya, yb, yc, w_gate16, w_gate16, w_gate16, b2, b2, b2, wa16, wb16, wc16)


def _out_proj_kernel(m_ref, x_ref, w_ref, g_ref, b_ref, o32_ref, o16_ref):
    r = ALPHA * x_ref[...] + jnp.dot(m_ref[...], w_ref[...], preferred_element_type=F32)
    y = _layer_norm_rows(r, g_ref[...], b_ref[...])
    o32_ref[...] = y
    o16_ref[...] = y.astype(BF16)


def _out_proj_call(merged, x, w_out16, g, b, tm=512):
    t, d = x.shape
    row = pl.BlockSpec((tm, d), lambda i: (i, 0))
    vec = pl.BlockSpec((1, d), lambda i: (0, 0))
    return pl.pallas_call(
        _out_proj_kernel,
        grid=(t // tm,),
        in_specs=[row, row, pl.BlockSpec((d, d), lambda i: (0, 0)), vec, vec],
        out_specs=[row, row],
        out_shape=[jax.ShapeDtypeStruct((t, d), F32), jax.ShapeDtypeStruct((t, d), BF16)],
        compiler_params=_params(("parallel",)),
        name="out_proj_ln",
    )(merged, x, w_out16, g.reshape(1, d), b.reshape(1, d))


def _dot_nt(a, b):
    return lax.dot_general(a, b, (((1,), (1,)), ((), ())), preferred_element_type=F32)


def _split_bf16(v):
    hi = v.astype(BF16)
    return hi, (v - hi.astype(F32)).astype(BF16)


def _to_groups(v):
    return jnp.stack([v[g * GROUP_SIZE:(g + 1) * GROUP_SIZE] for g in range(N_GROUPS)])


def _from_groups(v):
    return jnp.concatenate([v[g] for g in range(N_GROUPS)], axis=0)


def _over_experts(fn, v):
    return fn(fn(v, axis=0, keepdims=True), axis=1, keepdims=True)


def _router_kernel(x_ref, wt_ref, b_ref, wts_ref, pos_ref, cnt_ref, start_ref):
    tm = x_ref.shape[0]
    neg = -jnp.inf
    xh, xl = _split_bf16(x_ref[...])
    wh, wl = _split_bf16(wt_ref[...])
    logits = _dot_nt(wh, xh) + (_dot_nt(wh, xl) + _dot_nt(wl, xh))
    scores2d = jax.nn.sigmoid(logits)
    scores = _to_groups(scores2d)
    sel = _to_groups(scores2d + b_ref[...])
    shape3 = (N_GROUPS, GROUP_SIZE, tm)
    member = lax.broadcasted_iota(I32, shape3, 1)
    expert = lax.broadcasted_iota(I32, shape3, 0) * GROUP_SIZE + member

    m1 = jnp.max(sel, axis=1, keepdims=True)
    i1 = jnp.min(jnp.where(sel == m1, member, GROUP_SIZE), axis=1, keepdims=True)
    m2 = jnp.max(jnp.where(member == i1, neg, sel), axis=1, keepdims=True)
    gscore = m1 + m2

    gid = lax.broadcasted_iota(I32, (N_GROUPS, 1, tm), 0)
    keep = jnp.zeros((N_GROUPS, 1, tm), jnp.bool_)
    for _ in range(TOPK_GROUPS):
        m = jnp.max(gscore, axis=0, keepdims=True)
        gsel = jnp.min(jnp.where(gscore == m, gid, N_GROUPS), axis=0, keepdims=True)
        hit = gid == gsel
        keep = keep | hit
        gscore = jnp.where(hit, neg, gscore)

    cand = jnp.where(keep, sel, neg)
    chosen, weight = [], []
    for _ in range(TOP_K):
        m = _over_experts(jnp.max, cand)
        idx = _over_experts(jnp.min, jnp.where(cand == m, expert, N_EXPERTS))
        hit = expert == idx
        chosen.append(idx)
        weight.append(_over_experts(jnp.sum, jnp.where(hit, scores, 0.0))[0])
        cand = jnp.where(hit, neg, cand)
    norm = functools.reduce(jnp.add, weight)
    wts_ref[0] = jnp.concatenate(weight, axis=0) / norm * ROUTED_SCALE

    onehot = functools.reduce(jnp.add, [jnp.where(expert == idx, 1.0, 0.0) for idx in chosen])
    onehot = _from_groups(onehot)
    t_r = lax.broadcasted_iota(I32, (tm, tm), 0)
    t_c = lax.broadcasted_iota(I32, (tm, tm), 1)
    earlier = jnp.where(t_r < t_c, 1.0, 0.0).astype(BF16)
    rank = jnp.dot(onehot.astype(BF16), earlier, preferred_element_type=F32)
    count = jnp.sum(onehot, axis=1, keepdims=True)
    e_r = lax.broadcasted_iota(I32, (N_EXPERTS, N_EXPERTS), 0)
    e_c = lax.broadcasted_iota(I32, (N_EXPERTS, N_EXPERTS), 1)
    before = jnp.where(e_c < e_r, 1.0, 0.0).astype(BF16)
    start = jnp.dot(before, jnp.broadcast_to(count, (N_EXPERTS, tm)).astype(BF16),
                    preferred_element_type=F32)
    where_to = _to_groups(rank + start)
    pos = [_over_experts(jnp.sum, jnp.where(expert == idx, where_to, 0.0))[0] for idx in chosen]
    pos_ref[0] = jnp.concatenate(pos, axis=0).astype(I32)
    cnt_ref[0] = count.astype(I32)
    start_ref[0] = start[:, 0:1].astype(I32)


def _router_call(x, w_router, router_bias):
    t, d = x.shape
    nw = t // WINDOW
    per_tok = pl.BlockSpec((1, TOP_K, WINDOW), lambda i: (i, 0, 0))
    per_win = pl.BlockSpec((1, N_EXPERTS, 1), lambda i: (i, 0, 0))
    return pl.pallas_call(
        _router_kernel,
        grid=(nw,),
        in_specs=[pl.BlockSpec((WINDOW, d), lambda i: (i, 0)),
                  pl.BlockSpec((N_EXPERTS, d), lambda i: (0, 0)),
                  pl.BlockSpec((N_EXPERTS, 1), lambda i: (0, 0))],
        out_specs=[per_tok, per_tok, per_win, per_win],
        out_shape=[jax.ShapeDtypeStruct((nw, TOP_K, WINDOW), F32),
                   jax.ShapeDtypeStruct((nw, TOP_K, WINDOW), I32),
                   jax.ShapeDtypeStruct((nw, N_EXPERTS, 1), I32),
                   jax.ShapeDtypeStruct((nw, N_EXPERTS, 1), I32)],
        compiler_params=_params(("parallel",)),
        name="router",
    )(x, w_router.T, router_bias.reshape(N_EXPERTS, 1))


def _shared_kernel(x32_ref, x16_ref, wg_ref, wu_ref, wd_ref, o_ref):
    xb = x16_ref[...]
    a = jnp.dot(xb, wg_ref[...], preferred_element_type=F32)
    b = jnp.dot(xb, wu_ref[...], preferred_element_type=F32)
    hid = (jax.nn.silu(a) * b).astype(BF16)
    o_ref[...] = ALPHA * x32_ref[...] + jnp.dot(hid, wd_ref[...], preferred_element_type=F32)


def _shared_call(x32, x16, wg, wu, wd, tm=512):
    t, d = x32.shape
    hdim = wg.shape[1]
    row = pl.BlockSpec((tm, d), lambda i: (i, 0))
    return pl.pallas_call(
        _shared_kernel,
        grid=(t // tm,),
        in_specs=[row, row,
                  pl.BlockSpec((d, hdim), lambda i: (0, 0)),
                  pl.BlockSpec((d, hdim), lambda i: (0, 0)),
                  pl.BlockSpec((hdim, d), lambda i: (0, 0))],
        out_specs=row,
        out_shape=jax.ShapeDtypeStruct((t, d), F32),
        compiler_params=_params(("parallel",)),
        name="shared_expert",
    )(x32, x16, wg, wu, wd)


def _segment_copies(cnt_ref, start_ref, off_ref, win, local, remote, sem, to_remote, act):
    def body(e, carry):
        n = cnt_ref[win * N_EXPERTS + e]

        @pl.when(n > 0)
        def _():
            loc = local.at[pl.ds(start_ref[win * N_EXPERTS + e], n)]
            rem = remote.at[pl.ds(off_ref[win * N_EXPERTS + e], n)]
            act(pltpu.make_async_copy(loc, rem, sem) if to_remote
                else pltpu.make_async_copy(rem, loc, sem))
        return carry
    lax.fori_loop(0, N_EXPERTS, body, 0)


def _pad_copies(pad_start_ref, pad_len_ref, zeros, remote, sem, act):
    def body(e, carry):
        n = pad_len_ref[e]

        @pl.when(n > 0)
        def _():
            act(pltpu.make_async_copy(zeros.at[pl.ds(0, n)], remote.at[pl.ds(pad_start_ref[e], n)], sem))
        return carry
    lax.fori_loop(0, N_EXPERTS, body, 0)


def _dispatch_kernel(cnt_ref, start_ref, off_ref, pad_start_ref, pad_len_ref,
                     pos_ref, x_ref, rows_hbm, buf, zeros, sem, pad_sem):
    w = pl.program_id(0)
    last = pl.num_programs(0) - 1
    slot = w % 2
    start = lambda c: c.start()
    wait = lambda c: c.wait()
    copies = functools.partial(_segment_copies, cnt_ref, start_ref, off_ref,
                               remote=rows_hbm, to_remote=True)

    @pl.when(w == 0)
    def _():
        zeros[...] = jnp.zeros_like(zeros)
        _pad_copies(pad_start_ref, pad_len_ref, zeros, rows_hbm, pad_sem, start)

    @pl.when(w >= 2)
    def _():
        copies(win=w - 2, local=buf.at[slot], sem=sem.at[slot], act=wait)

    def place(t, carry):
        v = pltpu.bitcast(x_ref[t], U32)
        for k in range(TOP_K):
            buf[slot, pos_ref[0, 0, k * WINDOW + t]] = v
        return carry
    lax.fori_loop(0, WINDOW, place, 0, unroll=4)
    copies(win=w, local=buf.at[slot], sem=sem.at[slot], act=start)

    @pl.when(w == last)
    def _():
        @pl.when(w >= 1)
        def _():
            copies(win=w - 1, local=buf.at[1 - slot], sem=sem.at[1 - slot], act=wait)
        copies(win=w, local=buf.at[slot], sem=sem.at[slot], act=wait)
        _pad_copies(pad_start_ref, pad_len_ref, zeros, rows_hbm, pad_sem, wait)


def _dispatch_call(x_tiles, pos, cnt, start, off, pad_start, pad_len, n_rows):
    t = x_tiles.shape[0]
    nw = t // WINDOW
    grid_spec = pltpu.PrefetchScalarGridSpec(
        num_scalar_prefetch=5,
        grid=(nw,),
        in_specs=[pl.BlockSpec((1, 1, PAIRS), lambda w, *_: (w, 0, 0), memory_space=pltpu.SMEM),
                  pl.BlockSpec((WINDOW, ROW_TILES, LANES), lambda w, *_: (w, 0, 0))],
        out_specs=pl.BlockSpec(memory_space=pl.ANY),
        scratch_shapes=[pltpu.VMEM((2, PAIRS, WORD_ROWS, LANES), U32),
                        pltpu.VMEM((EXPERT_TILE, WORD_ROWS, LANES), U32),
                        pltpu.SemaphoreType.DMA((2,)),
                        pltpu.SemaphoreType.DMA(())],
    )
    return pl.pallas_call(
        _dispatch_kernel,
        grid_spec=grid_spec,
        out_shape=jax.ShapeDtypeStruct((n_rows, WORD_ROWS, LANES), U32),
        compiler_params=_params(("arbitrary",)),
        name="dispatch",
    )(cnt, start, off, pad_start, pad_len, pos, x_tiles)


def _expert_kernel(first_tile_ref, n_tiles_ref, rows_hbm, w1_ref, w3_ref, w2_ref, out_hbm,
                   w1b, w3b, w2b, xin, yout, xt, in_sem, out_sem):
    e = pl.program_id(0)
    last = pl.num_programs(0) - 1
    tr = xt.shape[0]
    blk = tr * WORD_ROWS
    g0 = first_tile_ref[e]
    n = n_tiles_ref[e]
    used = first_tile_ref[last] + n_tiles_ref[last]

    def tile_rows(g):
        return pl.ds(pl.multiple_of(g * blk, blk), blk)

    def in_copy(g, slot):
        return pltpu.make_async_copy(rows_hbm.at[tile_rows(g)], xin.at[slot], in_sem.at[slot])

    def out_copy(g, slot):
        return pltpu.make_async_copy(yout.at[slot], out_hbm.at[tile_rows(g)], out_sem.at[slot])

    @pl.when(e == 0)
    def _():
        in_copy(0, 0).start()

    @pl.when(n > 0)
    def _():
        w1b[...] = w1_ref[0, 0].astype(BF16)
        w3b[...] = w3_ref[0, 0].astype(BF16)
        w2b[...] = w2_ref[0, 0].astype(BF16)

    def tile(j, carry):
        g = g0 + j
        slot = g % 2
        in_copy(g, slot).wait()

        @pl.when(g + 1 < used)
        def _():
            in_copy(g + 1, 1 - slot).start()

        @pl.when(g >= 2)
        def _():
            out_copy(g - 2, slot).wait()

        for s in range(WORD_ROWS):
            low, high = _unpack_words(xin[slot, pl.ds(s, tr, stride=WORD_ROWS), :])
            xt[:, (2 * s) * LANES:(2 * s + 1) * LANES] = low.astype(BF16)
            xt[:, (2 * s + 1) * LANES:(2 * s + 2) * LANES] = high.astype(BF16)
        x = xt[...]
        a = jnp.dot(x, w1b[...], preferred_element_type=F32)
        b = jnp.dot(x, w3b[...], preferred_element_type=F32)
        hid = (jax.nn.silu(a) * b).astype(BF16)
        y = jnp.dot(hid, w2b[...], preferred_element_type=F32)
        for s in range(WORD_ROWS):
            yout[slot, pl.ds(s, tr, stride=WORD_ROWS), :] = _pack_words(
                y[:, (2 * s) * LANES:(2 * s + 1) * LANES], y[:, (2 * s + 1) * LANES:(2 * s + 2) * LANES])
        out_copy(g, slot).start()
        return carry
    lax.fori_loop(0, n, tile, 0)

    @pl.when(e == last)
    def _():
        @pl.when(used >= 2)
        def _():
            out_copy(used - 2, used % 2).wait()
        out_copy(used - 1, (used - 1) % 2).wait()


def _expert_call(rows2d, first_tile, n_tiles, layer, w1, w3, w2):
    _, ne, d, hdim = w1.shape
    blk = EXPERT_TILE * WORD_ROWS
    grid_spec = pltpu.PrefetchScalarGridSpec(
        num_scalar_prefetch=2,
        grid=(ne,),
        in_specs=[pl.BlockSpec(memory_space=pl.ANY),
                  pl.BlockSpec((1, 1, d, hdim), lambda e, *_: (layer, e, 0, 0)),
                  pl.BlockSpec((1, 1, d, hdim), lambda e, *_: (layer, e, 0, 0)),
                  pl.BlockSpec((1, 1, hdim, d), lambda e, *_: (layer, e, 0, 0))],
        out_specs=pl.BlockSpec(memory_space=pl.ANY),
        scratch_shapes=[pltpu.VMEM((d, hdim), BF16), pltpu.VMEM((d, hdim), BF16),
                        pltpu.VMEM((hdim, d), BF16),
                        pltpu.VMEM((2, blk, LANES), U32), pltpu.VMEM((2, blk, LANES), U32),
                        pltpu.VMEM((EXPERT_TILE, d), BF16),
                        pltpu.SemaphoreType.DMA((2,)), pltpu.SemaphoreType.DMA((2,))],
    )
    return pl.pallas_call(
        _expert_kernel,
        grid_spec=grid_spec,
        out_shape=jax.ShapeDtypeStruct(rows2d.shape, U32),
        compiler_params=_params(("arbitrary",)),
        name="routed_experts",
    )(first_tile, n_tiles, rows2d, w1, w3, w2)


def _combine_kernel(cnt_ref, start_ref, off_ref, pos_ref, wts_ref, rows_hbm, s_ref, g_ref, b_ref,
                    o32_ref, o16_ref, buf, acc_lo, acc_hi, sem):
    w = pl.program_id(0)
    last = pl.num_programs(0) - 1
    slot = w % 2
    start = lambda c: c.start()
    wait = lambda c: c.wait()
    copies = functools.partial(_segment_copies, cnt_ref, start_ref, off_ref,
                               remote=rows_hbm, to_remote=False)

    @pl.when(w == 0)
    def _():
        copies(win=w, local=buf.at[slot], sem=sem.at[slot], act=start)

    @pl.when(w < last)
    def _():
        copies(win=w + 1, local=buf.at[1 - slot], sem=sem.at[1 - slot], act=start)

    copies(win=w, local=buf.at[slot], sem=sem.at[slot], act=wait)

    def token(t, carry):
        lo, hi = _unpack_words(buf[slot, pos_ref[0, 0, t]])
        wt = wts_ref[0, 0, t]
        lo, hi = wt * lo, wt * hi
        for k in range(1, TOP_K):
            l_k, h_k = _unpack_words(buf[slot, pos_ref[0, 0, k * WINDOW + t]])
            wt = wts_ref[0, 0, k * WINDOW + t]
            lo, hi = lo + wt * l_k, hi + wt * h_k
        at = pl.ds(pl.multiple_of(t * WORD_ROWS, WORD_ROWS), WORD_ROWS)
        acc_lo[at, :] = lo
        acc_hi[at, :] = hi
        return carry
    lax.fori_loop(0, WINDOW, token, 0, unroll=2)

    chunks = []
    for s in range(WORD_ROWS):
        chunks.append(acc_lo[pl.ds(s, WINDOW, stride=WORD_ROWS), :])
        chunks.append(acc_hi[pl.ds(s, WINDOW, stride=WORD_ROWS), :])
    out = _layer_norm_rows(s_ref[...] + jnp.concatenate(chunks, axis=1), g_ref[...], b_ref[...])
    o32_ref[...] = out
    o16_ref[...] = out.astype(BF16)


def _combine_call(rows_tiles, pos, wts, cnt, start, off, s, g, b):
    t, d = s.shape
    nw = t // WINDOW
    smem_block = pl.BlockSpec((1, 1, PAIRS), lambda w, *_: (w, 0, 0), memory_space=pltpu.SMEM)
    row = pl.BlockSpec((WINDOW, d), lambda w, *_: (w, 0))
    vec = pl.BlockSpec((1, d), lambda w, *_: (0, 0))
    grid_spec = pltpu.PrefetchScalarGridSpec(
        num_scalar_prefetch=3,
        grid=(nw,),
        in_specs=[smem_block, smem_block, pl.BlockSpec(memory_space=pl.ANY), row, vec, vec],
        out_specs=[row, row],
        scratch_shapes=[pltpu.VMEM((2, PAIRS, WORD_ROWS, LANES), U32),
                        pltpu.VMEM((WINDOW * WORD_ROWS, LANES), F32),
                        pltpu.VMEM((WINDOW * WORD_ROWS, LANES), F32),
                        pltpu.SemaphoreType.DMA((2,))],
    )
    return pl.pallas_call(
        _combine_kernel,
        grid_spec=grid_spec,
        out_shape=[jax.ShapeDtypeStruct((t, d), F32), jax.ShapeDtypeStruct((t, d), BF16)],
        compiler_params=_params(("arbitrary",)),
        name="combine_ln",
    )(cnt, start, off, pos, wts, rows_tiles, s, g.reshape(1, d), b.reshape(1, d))


def _row_layout(cnt):
    cnt2 = cnt.reshape(-1, N_EXPERTS)
    total = jnp.sum(cnt2, axis=0)
    padded = (total + EXPERT_TILE - 1) // EXPERT_TILE * EXPERT_TILE
    base = jnp.cumsum(padded) - padded
    off = base[None, :] + jnp.cumsum(cnt2, axis=0) - cnt2
    return (off.reshape(-1).astype(I32), (base + total).astype(I32), (padded - total).astype(I32),
            (base // EXPERT_TILE).astype(I32), (padded // EXPERT_TILE).astype(I32))


def _moe_block(x32, x16, layer, w_router, router_bias, w1, w3, w2, ws1, ws3, ws2, g, b):
    t, d = x32.shape
    nw = t // WINDOW
    n_tiles = t * TOP_K // EXPERT_TILE + N_EXPERTS
    n_rows = n_tiles * EXPERT_TILE
    wts, pos, cnt, start = _router_call(x32, w_router, router_bias)
    off, pad_start, pad_len, first_tile, tiles_of = _row_layout(cnt)
    cnt = cnt.reshape(-1)
    start = start.reshape(-1)
    pos = pos.reshape(nw, 1, PAIRS)
    wts = wts.reshape(nw, 1, PAIRS)
    rows = _dispatch_call(x16.reshape(t, ROW_TILES, LANES), pos, cnt, start, off, pad_start, pad_len, n_rows)
    s = _shared_call(x32, x16, ws1.astype(BF16), ws3.astype(BF16), ws2.astype(BF16))
    out_rows = _expert_call(rows.reshape(n_rows * WORD_ROWS, LANES), first_tile, tiles_of, layer, w1, w3, w2)
    return _combine_call(out_rows.reshape(n_rows, WORD_ROWS, LANES), pos, wts, cnt, start, off, s, g, b)


def kernel(x, in_ln_g, in_ln_b, w_in, w_spatial, b_spatial, w_conv, w_pool, pool_scale,
           w_gate, b_gate, w_proj_a, w_proj_b, w_proj_c, w_out, ln1_g, ln1_b,
           w_router, router_bias, w_expert_gate, w_expert_up, w_expert_down,
           w_shared_gate, w_shared_up, w_shared_down, ln2_g, ln2_b):
    bsz, seq, d = x.shape
    x32, x16 = _ln_call(x.reshape(bsz * seq, d), in_ln_g, in_ln_b)
    for l in range(DEPTH):
        ya, yb, yc = _mix_tokens_call(x16, w_in[l].astype(BF16), w_spatial[l], b_spatial[l], w_conv[l],
                                      w_pool[l].astype(BF16), pool_scale[l])
        merged = _merge_call(x16, ya, yb, yc, w_gate[l].astype(BF16), b_gate[l],
                             w_proj_a[l].astype(BF16), w_proj_b[l].astype(BF16), w_proj_c[l].astype(BF16))
        x32, x16 = _out_proj_call(merged, x32, w_out[l].astype(BF16), ln1_g[l], ln1_b[l])
        x32, x16 = _moe_block(x32, x16, l, w_router[l], router_bias[l], w_expert_gate, w_expert_up,
                              w_expert_down, w_shared_gate[l], w_shared_up[l], w_shared_down[l],
                              ln2_g[l], ln2_b[l])
    return x32.reshape(bsz, seq, d)
```

```python
import functools

import jax
import jax.numpy as jnp
from jax import lax
from jax.experimental import pallas as pl
from jax.experimental.pallas import tpu as pltpu

F32 = jnp.float32
BF16 = jnp.bfloat16
I32 = jnp.int32
U32 = jnp.uint32

D_MODEL = 2048
DEPTH = 2
CHUNK = 64
A_WIDTH = 1024
A_BLOCK = 128
N_HEAD_GROUPS = 4
HEAD = A_WIDTH // N_HEAD_GROUPS
N_STREAMS = 6
CONV_W = 3
N_BRANCH = 3
N_EXPERTS = 64
TOP_K = 8
N_GROUPS = 8
GROUP_SIZE = N_EXPERTS // N_GROUPS
TOPK_GROUPS = 4
ROUTED_SCALE = 2.5
ALPHA = (2.0 * DEPTH) ** 0.25
LN_EPS = 1e-5

LANES = 128
ROW_TILES = D_MODEL // LANES
WORD_ROWS = ROW_TILES // 2
MERGE_COLS = 512
WINDOW = 256
PAIRS = WINDOW * TOP_K
EXPERT_TILE = 512
VMEM_LIMIT = 56 * 1024 * 1024
HIGH_HALF = 0xFFFF0000

def _params(sem, vmem=VMEM_LIMIT):
    return pltpu.CompilerParams(dimension_semantics=sem, vmem_limit_bytes=vmem)


def _gelu(x):
    return 0.5 * x * (1.0 + lax.erf(x * (0.5 ** 0.5)))


def _layer_norm_rows(r, g, b):
    mu = jnp.mean(r, axis=-1, keepdims=True)
    c = r - mu
    var = jnp.mean(c * c, axis=-1, keepdims=True)
    return c * lax.rsqrt(var + LN_EPS) * g + b


def _unpack_words(words):
    low = pltpu.bitcast(lax.shift_left(words, jnp.uint32(16)), F32)
    high = pltpu.bitcast(words & jnp.uint32(HIGH_HALF), F32)
    return low, high


def _pack_words(low, high):
    lo = pltpu.bitcast(low.astype(BF16).astype(F32), U32)
    hi = pltpu.bitcast(high.astype(BF16).astype(F32), U32)
    return lax.shift_right_logical(lo, jnp.uint32(16)) | (hi & jnp.uint32(HIGH_HALF))


def _ln_kernel(x_ref, g_ref, b_ref, o32_ref, o16_ref):
    y = _layer_norm_rows(x_ref[...], g_ref[...], b_ref[...])
    o32_ref[...] = y
    o16_ref[...] = y.astype(BF16)


def _ln_call(x, g, b, tm=256):
    t, d = x.shape
    row = pl.BlockSpec((tm, d), lambda i: (i, 0))
    vec = pl.BlockSpec((1, d), lambda i: (0, 0))
    return pl.pallas_call(
        _ln_kernel,
        grid=(t // tm,),
        in_specs=[row, vec, vec],
        out_specs=[row, row],
        out_shape=[jax.ShapeDtypeStruct((t, d), F32), jax.ShapeDtypeStruct((t, d), BF16)],
        compiler_params=_params(("parallel",)),
        name="layer_norm",
    )(x, g.reshape(1, d), b.reshape(1, d))


def _mix_tokens_kernel(x_ref, wu_ref, wv_ref, wbg_ref, wcg_ref, wh_ref, wpp_ref,
                       ws_ref, bs_ref, wc_ref, wp_ref, ps_ref,
                       ya_ref, yb_ref, yc_ref, zc_ref, pc_ref):
    g = pl.program_id(0)
    i = pl.program_id(1)
    tm = x_ref.shape[0]

    @pl.when(i == 0)
    def _():
        zc_ref[...] = jnp.zeros_like(zc_ref)
        pc_ref[...] = jnp.zeros_like(pc_ref)

    x = x_ref[...]
    proj = lambda w_ref: jnp.dot(x, w_ref[...], preferred_element_type=F32)

    u = _gelu(proj(wu_ref))
    v = _gelu(proj(wv_ref))
    mu = jnp.mean(v, axis=-1, keepdims=True)
    vc = v - mu
    var = jnp.mean(vc * vc, axis=-1, keepdims=True)
    vn = (vc * lax.rsqrt(var + LN_EPS)).astype(BF16)
    qi = lax.broadcasted_iota(I32, (A_BLOCK, A_BLOCK), 0) // CHUNK
    kj = lax.broadcasted_iota(I32, (A_BLOCK, A_BLOCK), 1) // CHUNK
    ws = jnp.where(kj <= qi, ws_ref[0], 0.0).astype(BF16)
    bs = bs_ref[0]
    for n in range(tm // A_BLOCK):
        rows = slice(n * A_BLOCK, (n + 1) * A_BLOCK)
        mixed = jnp.dot(ws, vn[rows], preferred_element_type=F32) + bs
        ya_ref[rows, :] = (u[rows] * mixed).astype(BF16)

    z = proj(wcg_ref) * proj(wh_ref)
    row = lax.broadcasted_iota(I32, (tm, 1), 0)
    prev1 = zc_ref[7:8, :]
    prev2 = zc_ref[6:7, :]
    z1 = jnp.where(row == 0, prev1, pltpu.roll(z, 1, 0))
    z2 = jnp.where(row == 0, prev2, jnp.where(row == 1, prev1, pltpu.roll(z, 2, 0)))
    conv = wc_ref[0:1, :] * z2 + wc_ref[1:2, :] * z1 + wc_ref[2:3, :] * z
    yb_ref[...] = (proj(wbg_ref) * conv).astype(BF16)
    zc_ref[...] = z[tm - 8:, :]

    p = proj(wpp_ref)
    win = lax.shift_left(jnp.int32(2), g)
    p16 = p.astype(BF16)
    pext = jnp.concatenate([pc_ref[...], p16], axis=0)
    rr = lax.broadcasted_iota(I32, (tm, tm + A_BLOCK), 0) + A_BLOCK
    cc = lax.broadcasted_iota(I32, (tm, tm + A_BLOCK), 1)
    band = jnp.where((cc <= rr) & (cc > rr - win), 1.0, 0.0).astype(BF16)
    total = jnp.dot(band, pext, preferred_element_type=F32)
    count = jnp.minimum(i * tm + row + 1, win).astype(F32)
    pooled = (total / count - p).astype(BF16)
    yc = jnp.dot(pooled, wp_ref[0], preferred_element_type=F32) * ps_ref[...]
    yc_ref[...] = yc.astype(BF16)
    pc_ref[...] = p16[tm - A_BLOCK:, :]


def _mix_tokens_call(xb, w_in16, w_s, b_s, w_conv, w_pool16, pool_scale, tm=512):
    t, d = xb.shape
    stream = lambda s: pl.BlockSpec((d, HEAD), lambda g, i: (0, s * N_HEAD_GROUPS + g))
    grp = lambda *shape: pl.BlockSpec((1,) + shape, lambda g, i: (g,) + (0,) * len(shape))
    out_spec = pl.BlockSpec((tm, HEAD), lambda g, i: (i, g))
    out_shape = jax.ShapeDtypeStruct((t, A_WIDTH), BF16)
    return pl.pallas_call(
        _mix_tokens_kernel,
        grid=(N_HEAD_GROUPS, t // tm),
        in_specs=[pl.BlockSpec((tm, d), lambda g, i: (i, 0))]
        + [stream(s) for s in range(N_STREAMS)]
        + [grp(A_BLOCK, A_BLOCK), grp(A_BLOCK, 1),
           pl.BlockSpec((CONV_W, HEAD), lambda g, i: (0, g)),
           grp(HEAD, HEAD),
           pl.BlockSpec((1, HEAD), lambda g, i: (0, g))],
        out_specs=[out_spec, out_spec, out_spec],
        out_shape=[out_shape, out_shape, out_shape],
        scratch_shapes=[pltpu.VMEM((8, HEAD), F32), pltpu.VMEM((A_BLOCK, HEAD), BF16)],
        compiler_params=_params(("arbitrary", "arbitrary")),
        name="mix_tokens",
    )(xb, *([w_in16] * N_STREAMS), w_s, b_s.reshape(N_HEAD_GROUPS, A_BLOCK, 1), w_conv,
      w_pool16, pool_scale.reshape(1, A_WIDTH))


def _merge_kernel(x_ref, ya_ref, yb_ref, yc_ref, wg0_ref, wg1_ref, wg2_ref, bg0_ref, bg1_ref, bg2_ref,
                  wa_ref, wb_ref, wc_ref, m_ref):
    x = x_ref[...]
    gate = lambda w_ref, b_ref: jax.nn.sigmoid(
        jnp.dot(x, w_ref[...], preferred_element_type=F32) + b_ref[...])
    m = gate(wg0_ref, bg0_ref) * jnp.dot(ya_ref[...], wa_ref[...], preferred_element_type=F32)
    m = m + gate(wg1_ref, bg1_ref) * jnp.dot(yb_ref[...], wb_ref[...], preferred_element_type=F32)
    m = m + gate(wg2_ref, bg2_ref) * jnp.dot(yc_ref[...], wc_ref[...], preferred_element_type=F32)
    m_ref[...] = m.astype(BF16)


def _merge_call(xb, ya, yb, yc, w_gate16, b_gate, wa16, wb16, wc16, tm=512):
    t, d = xb.shape
    nj = d // MERGE_COLS
    tok = lambda width: pl.BlockSpec((tm, width), lambda j, i: (i, 0))
    gate_w = lambda br: pl.BlockSpec((d, MERGE_COLS), lambda j, i: (0, br * nj + j))
    gate_b = lambda br: pl.BlockSpec((1, MERGE_COLS), lambda j, i: (0, br * nj + j))
    branch_w = pl.BlockSpec((A_WIDTH, MERGE_COLS), lambda j, i: (0, j))
    b2 = b_gate.reshape(1, N_BRANCH * d)
    return pl.pallas_call(
        _merge_kernel,
        grid=(nj, t // tm),
        in_specs=[tok(d), tok(A_WIDTH), tok(A_WIDTH), tok(A_WIDTH)]
        + [gate_w(br) for br in range(N_BRANCH)] + [gate_b(br) for br in range(N_BRANCH)]
        + [branch_w] * N_BRANCH,
        out_specs=pl.BlockSpec((tm, MERGE_COLS), lambda j, i: (i, j)),
        out_shape=jax.ShapeDtypeStruct((t, d), BF16),
        compiler_params=_params(("arbitrary", "arbitrary")),
        name="merge_branches",
    )(xb, ya, yb, yc, w_gate16, w_gate16, w_gate16, b2, b2, b2, wa16, wb16, wc16)


def _out_proj_kernel(m_ref, x_ref, w_ref, g_ref, b_ref, wg_ref, wu_ref, wd_ref, o32_ref, o16_ref, s_ref):
    r = ALPHA * x_ref[...] + jnp.dot(m_ref[...], w_ref[...], preferred_element_type=F32)
    y = _layer_norm_rows(r, g_ref[...], b_ref[...])
    yb = y.astype(BF16)
    o32_ref[...] = y
    o16_ref[...] = yb
    a = jnp.dot(yb, wg_ref[...], preferred_element_type=F32)
    b = jnp.dot(yb, wu_ref[...], preferred_element_type=F32)
    hid = (jax.nn.silu(a) * b).astype(BF16)
    s_ref[...] = ALPHA * y + jnp.dot(hid, wd_ref[...], preferred_element_type=F32)


def _out_proj_call(merged, x, w_out16, g, b, wg16, wu16, wd16, tm=256):
    t, d = x.shape
    hdim = wg16.shape[1]
    row = pl.BlockSpec((tm, d), lambda i: (i, 0))
    vec = pl.BlockSpec((1, d), lambda i: (0, 0))
    whole = lambda r, c: pl.BlockSpec((r, c), lambda i: (0, 0))
    return pl.pallas_call(
        _out_proj_kernel,
        grid=(t // tm,),
        in_specs=[row, row, whole(d, d), vec, vec, whole(d, hdim), whole(d, hdim), whole(hdim, d)],
        out_specs=[row, row, row],
        out_shape=[jax.ShapeDtypeStruct((t, d), F32), jax.ShapeDtypeStruct((t, d), BF16),
                   jax.ShapeDtypeStruct((t, d), F32)],
        compiler_params=_params(("parallel",)),
        name="out_proj_ln_shared",
    )(merged, x, w_out16, g.reshape(1, d), b.reshape(1, d), wg16, wu16, wd16)


def _dot_nt(a, b):
    return lax.dot_general(a, b, (((1,), (1,)), ((), ())), preferred_element_type=F32)


def _split_bf16(v):
    hi = v.astype(BF16)
    return hi, (v - hi.astype(F32)).astype(BF16)


def _to_groups(v):
    return jnp.stack([v[g * GROUP_SIZE:(g + 1) * GROUP_SIZE] for g in range(N_GROUPS)])


def _from_groups(v):
    return jnp.concatenate([v[g] for g in range(N_GROUPS)], axis=0)


def _over_experts(fn, v):
    return fn(fn(v, axis=0, keepdims=True), axis=1, keepdims=True)


def _router_kernel(x_ref, wt_ref, b_ref, wts_ref, pos_ref, cnt_ref, start_ref):
    tm = x_ref.shape[0]
    neg = -jnp.inf
    xh, xl = _split_bf16(x_ref[...])
    wh, wl = _split_bf16(wt_ref[...])
    logits = _dot_nt(wh, xh) + (_dot_nt(wh, xl) + _dot_nt(wl, xh))
    scores2d = jax.nn.sigmoid(logits)
    scores = _to_groups(scores2d)
    sel = _to_groups(scores2d + b_ref[...])
    shape3 = (N_GROUPS, GROUP_SIZE, tm)
    member = lax.broadcasted_iota(I32, shape3, 1)
    expert = lax.broadcasted_iota(I32, shape3, 0) * GROUP_SIZE + member

    m1 = jnp.max(sel, axis=1, keepdims=True)
    i1 = jnp.min(jnp.where(sel == m1, member, GROUP_SIZE), axis=1, keepdims=True)
    m2 = jnp.max(jnp.where(member == i1, neg, sel), axis=1, keepdims=True)
    gscore = m1 + m2

    gid = lax.broadcasted_iota(I32, (N_GROUPS, 1, tm), 0)
    keep = jnp.zeros((N_GROUPS, 1, tm), jnp.bool_)
    for _ in range(TOPK_GROUPS):
        m = jnp.max(gscore, axis=0, keepdims=True)
        gsel = jnp.min(jnp.where(gscore == m, gid, N_GROUPS), axis=0, keepdims=True)
        hit = gid == gsel
        keep = keep | hit
        gscore = jnp.where(hit, neg, gscore)

    cand = jnp.where(keep, sel, neg)
    chosen, weight = [], []
    for _ in range(TOP_K):
        m = _over_experts(jnp.max, cand)
        idx = _over_experts(jnp.min, jnp.where(cand == m, expert, N_EXPERTS))
        hit = expert == idx
        chosen.append(idx)
        weight.append(_over_experts(jnp.sum, jnp.where(hit, scores, 0.0))[0])
        cand = jnp.where(hit, neg, cand)
    norm = functools.reduce(jnp.add, weight)
    wts_ref[0] = jnp.concatenate(weight, axis=0) / norm * ROUTED_SCALE

    onehot = functools.reduce(jnp.add, [jnp.where(expert == idx, 1.0, 0.0) for idx in chosen])
    onehot = _from_groups(onehot)
    t_r = lax.broadcasted_iota(I32, (tm, tm), 0)
    t_c = lax.broadcasted_iota(I32, (tm, tm), 1)
    earlier = jnp.where(t_r < t_c, 1.0, 0.0).astype(BF16)
    rank = jnp.dot(onehot.astype(BF16), earlier, preferred_element_type=F32)
    count = jnp.sum(onehot, axis=1, keepdims=True)
    e_r = lax.broadcasted_iota(I32, (N_EXPERTS, N_EXPERTS), 0)
    e_c = lax.broadcasted_iota(I32, (N_EXPERTS, N_EXPERTS), 1)
    before = jnp.where(e_c < e_r, 1.0, 0.0).astype(BF16)
    c_hi, c_lo = _split_bf16(jnp.broadcast_to(count, (N_EXPERTS, tm)))
    start = (jnp.dot(before, c_hi, preferred_element_type=F32)
             + jnp.dot(before, c_lo, preferred_element_type=F32))
    where_to = _to_groups(rank + start)
    pos = [_over_experts(jnp.sum, jnp.where(expert == idx, where_to, 0.0))[0] for idx in chosen]
    pos_ref[0] = jnp.concatenate(pos, axis=0).astype(I32)
    cnt_ref[0] = count.astype(I32)
    start_ref[0] = start[:, 0:1].astype(I32)


def _router_call(x, w_router, router_bias):
    t, d = x.shape
    nw = t // WINDOW
    per_tok = pl.BlockSpec((1, TOP_K, WINDOW), lambda i: (i, 0, 0))
    per_win = pl.BlockSpec((1, N_EXPERTS, 1), lambda i: (i, 0, 0))
    return pl.pallas_call(
        _router_kernel,
        grid=(nw,),
        in_specs=[pl.BlockSpec((WINDOW, d), lambda i: (i, 0)),
                  pl.BlockSpec((N_EXPERTS, d), lambda i: (0, 0)),
                  pl.BlockSpec((N_EXPERTS, 1), lambda i: (0, 0))],
        out_specs=[per_tok, per_tok, per_win, per_win],
        out_shape=[jax.ShapeDtypeStruct((nw, TOP_K, WINDOW), F32),
                   jax.ShapeDtypeStruct((nw, TOP_K, WINDOW), I32),
                   jax.ShapeDtypeStruct((nw, N_EXPERTS, 1), I32),
                   jax.ShapeDtypeStruct((nw, N_EXPERTS, 1), I32)],
        compiler_params=_params(("parallel",)),
        name="router",
    )(x, w_router.T, router_bias.reshape(N_EXPERTS, 1))


def _segment_copies(cnt_ref, start_ref, off_ref, win, local, remote, sem, to_remote, act):
    def body(e, carry):
        n = cnt_ref[win * N_EXPERTS + e]

        @pl.when(n > 0)
        def _():
            loc = local.at[pl.ds(start_ref[win * N_EXPERTS + e], n)]
            rem = remote.at[pl.ds(off_ref[win * N_EXPERTS + e], n)]
            act(pltpu.make_async_copy(loc, rem, sem) if to_remote
                else pltpu.make_async_copy(rem, loc, sem))
        return carry
    lax.fori_loop(0, N_EXPERTS, body, 0)


def _pad_copies(pad_start_ref, pad_len_ref, zeros, remote, sem, act):
    def body(e, carry):
        n = pad_len_ref[e]

        @pl.when(n > 0)
        def _():
            act(pltpu.make_async_copy(zeros.at[pl.ds(0, n)], remote.at[pl.ds(pad_start_ref[e], n)], sem))
        return carry
    lax.fori_loop(0, N_EXPERTS, body, 0)


def _dispatch_kernel(cnt_ref, start_ref, off_ref, pad_start_ref, pad_len_ref,
                     pos_ref, x_ref, rows_hbm, buf, zeros, sem, pad_sem):
    w = pl.program_id(0)
    last = pl.num_programs(0) - 1
    slot = w % 2
    start = lambda c: c.start()
    wait = lambda c: c.wait()
    copies = functools.partial(_segment_copies, cnt_ref, start_ref, off_ref,
                               remote=rows_hbm, to_remote=True)

    @pl.when(w == 0)
    def _():
        zeros[...] = jnp.zeros_like(zeros)
        _pad_copies(pad_start_ref, pad_len_ref, zeros, rows_hbm, pad_sem, start)

    @pl.when(w >= 2)
    def _():
        copies(win=w - 2, local=buf.at[slot], sem=sem.at[slot], act=wait)

    def place(t, carry):
        v = pltpu.bitcast(x_ref[t], U32)
        for k in range(TOP_K):
            buf[slot, pos_ref[0, 0, k * WINDOW + t]] = v
        return carry
    lax.fori_loop(0, WINDOW, place, 0, unroll=4)
    copies(win=w, local=buf.at[slot], sem=sem.at[slot], act=start)

    @pl.when(w == last)
    def _():
        @pl.when(w >= 1)
        def _():
            copies(win=w - 1, local=buf.at[1 - slot], sem=sem.at[1 - slot], act=wait)
        copies(win=w, local=buf.at[slot], sem=sem.at[slot], act=wait)
        _pad_copies(pad_start_ref, pad_len_ref, zeros, rows_hbm, pad_sem, wait)


def _dispatch_call(x_tiles, pos, cnt, start, off, pad_start, pad_len, n_rows):
    t = x_tiles.shape[0]
    nw = t // WINDOW
    grid_spec = pltpu.PrefetchScalarGridSpec(
        num_scalar_prefetch=5,
        grid=(nw,),
        in_specs=[pl.BlockSpec((1, 1, PAIRS), lambda w, *_: (w, 0, 0), memory_space=pltpu.SMEM),
                  pl.BlockSpec((WINDOW, ROW_TILES, LANES), lambda w, *_: (w, 0, 0))],
        out_specs=pl.BlockSpec(memory_space=pl.ANY),
        scratch_shapes=[pltpu.VMEM((2, PAIRS, WORD_ROWS, LANES), U32),
                        pltpu.VMEM((EXPERT_TILE, WORD_ROWS, LANES), U32),
                        pltpu.SemaphoreType.DMA((2,)),
                        pltpu.SemaphoreType.DMA(())],
    )
    return pl.pallas_call(
        _dispatch_kernel,
        grid_spec=grid_spec,
        out_shape=jax.ShapeDtypeStruct((n_rows, WORD_ROWS, LANES), U32),
        compiler_params=_params(("arbitrary",)),
        name="dispatch",
    )(cnt, start, off, pad_start, pad_len, pos, x_tiles)


def _expert_kernel(first_tile_ref, n_tiles_ref, rows_hbm, w1_ref, w3_ref, w2_ref, out_hbm,
                   w1b, w3b, w2b, xin, yout, xt, in_sem, out_sem):
    e = pl.program_id(0)
    last = pl.num_programs(0) - 1
    tr = xt.shape[0]
    blk = tr * WORD_ROWS
    g0 = first_tile_ref[e]
    n = n_tiles_ref[e]
    used = first_tile_ref[last] + n_tiles_ref[last]

    def tile_rows(g):
        return pl.ds(pl.multiple_of(g * blk, blk), blk)

    def in_copy(g, slot):
        return pltpu.make_async_copy(rows_hbm.at[tile_rows(g)], xin.at[slot], in_sem.at[slot])

    def out_copy(g, slot):
        return pltpu.make_async_copy(yout.at[slot], out_hbm.at[tile_rows(g)], out_sem.at[slot])

    @pl.when(e == 0)
    def _():
        in_copy(0, 0).start()

    @pl.when(n > 0)
    def _():
        w1b[...] = w1_ref[0, 0].astype(BF16)
        w3b[...] = w3_ref[0, 0].astype(BF16)
        w2b[...] = w2_ref[0, 0].astype(BF16)

    def tile(j, carry):
        g = g0 + j
        slot = g % 2
        in_copy(g, slot).wait()

        @pl.when(g + 1 < used)
        def _():
            in_copy(g + 1, 1 - slot).start()

        @pl.when(g >= 2)
        def _():
            out_copy(g - 2, slot).wait()

        for s in range(WORD_ROWS):
            low, high = _unpack_words(xin[slot, pl.ds(s, tr, stride=WORD_ROWS), :])
            xt[:, (2 * s) * LANES:(2 * s + 1) * LANES] = low.astype(BF16)
            xt[:, (2 * s + 1) * LANES:(2 * s + 2) * LANES] = high.astype(BF16)
        x = xt[...]
        a = jnp.dot(x, w1b[...], preferred_element_type=F32)
        b = jnp.dot(x, w3b[...], preferred_element_type=F32)
        hid = (jax.nn.silu(a) * b).astype(BF16)
        y = jnp.dot(hid, w2b[...], preferred_element_type=F32)
        for s in range(WORD_ROWS):
            yout[slot, pl.ds(s, tr, stride=WORD_ROWS), :] = _pack_words(
                y[:, (2 * s) * LANES:(2 * s + 1) * LANES], y[:, (2 * s + 1) * LANES:(2 * s + 2) * LANES])
        out_copy(g, slot).start()
        return carry
    lax.fori_loop(0, n, tile, 0)

    @pl.when(e == last)
    def _():
        @pl.when(used >= 2)
        def _():
            out_copy(used - 2, used % 2).wait()
        out_copy(used - 1, (used - 1) % 2).wait()


def _expert_call(rows2d, first_tile, n_tiles, layer, w1, w3, w2):
    _, ne, d, hdim = w1.shape
    blk = EXPERT_TILE * WORD_ROWS
    grid_spec = pltpu.PrefetchScalarGridSpec(
        num_scalar_prefetch=2,
        grid=(ne,),
        in_specs=[pl.BlockSpec(memory_space=pl.ANY),
                  pl.BlockSpec((1, 1, d, hdim), lambda e, *_: (layer, e, 0, 0)),
                  pl.BlockSpec((1, 1, d, hdim), lambda e, *_: (layer, e, 0, 0)),
                  pl.BlockSpec((1, 1, hdim, d), lambda e, *_: (layer, e, 0, 0))],
        out_specs=pl.BlockSpec(memory_space=pl.ANY),
        scratch_shapes=[pltpu.VMEM((d, hdim), BF16), pltpu.VMEM((d, hdim), BF16),
                        pltpu.VMEM((hdim, d), BF16),
                        pltpu.VMEM((2, blk, LANES), U32), pltpu.VMEM((2, blk, LANES), U32),
                        pltpu.VMEM((EXPERT_TILE, d), BF16),
                        pltpu.SemaphoreType.DMA((2,)), pltpu.SemaphoreType.DMA((2,))],
    )
    return pl.pallas_call(
        _expert_kernel,
        grid_spec=grid_spec,
        out_shape=jax.ShapeDtypeStruct(rows2d.shape, U32),
        compiler_params=_params(("arbitrary",)),
        name="routed_experts",
    )(first_tile, n_tiles, rows2d, w1, w3, w2)


def _combine_kernel(cnt_ref, start_ref, off_ref, pos_ref, wts_ref, rows_hbm, s_ref, g_ref, b_ref,
                    o32_ref, o16_ref, buf, acc_lo, acc_hi, sem):
    w = pl.program_id(0)
    last = pl.num_programs(0) - 1
    slot = w % 2
    start = lambda c: c.start()
    wait = lambda c: c.wait()
    copies = functools.partial(_segment_copies, cnt_ref, start_ref, off_ref,
                               remote=rows_hbm, to_remote=False)

    @pl.when(w == 0)
    def _():
        copies(win=w, local=buf.at[slot], sem=sem.at[slot], act=start)

    @pl.when(w < last)
    def _():
        copies(win=w + 1, local=buf.at[1 - slot], sem=sem.at[1 - slot], act=start)

    copies(win=w, local=buf.at[slot], sem=sem.at[slot], act=wait)

    def token(t, carry):
        lo, hi = _unpack_words(buf[slot, pos_ref[0, 0, t]])
        wt = wts_ref[0, 0, t]
        lo, hi = wt * lo, wt * hi
        for k in range(1, TOP_K):
            l_k, h_k = _unpack_words(buf[slot, pos_ref[0, 0, k * WINDOW + t]])
            wt = wts_ref[0, 0, k * WINDOW + t]
            lo, hi = lo + wt * l_k, hi + wt * h_k
        at = pl.ds(pl.multiple_of(t * WORD_ROWS, WORD_ROWS), WORD_ROWS)
        acc_lo[at, :] = lo
        acc_hi[at, :] = hi
        return carry
    lax.fori_loop(0, WINDOW, token, 0, unroll=2)

    chunks = []
    for s in range(WORD_ROWS):
        chunks.append(acc_lo[pl.ds(s, WINDOW, stride=WORD_ROWS), :])
        chunks.append(acc_hi[pl.ds(s, WINDOW, stride=WORD_ROWS), :])
    out = _layer_norm_rows(s_ref[...] + jnp.concatenate(chunks, axis=1), g_ref[...], b_ref[...])
    o32_ref[...] = out
    o16_ref[...] = out.astype(BF16)


def _combine_call(rows_tiles, pos, wts, cnt, start, off, s, g, b):
    t, d = s.shape
    nw = t // WINDOW
    smem_block = pl.BlockSpec((1, 1, PAIRS), lambda w, *_: (w, 0, 0), memory_space=pltpu.SMEM)
    row = pl.BlockSpec((WINDOW, d), lambda w, *_: (w, 0))
    vec = pl.BlockSpec((1, d), lambda w, *_: (0, 0))
    grid_spec = pltpu.PrefetchScalarGridSpec(
        num_scalar_prefetch=3,
        grid=(nw,),
        in_specs=[smem_block, smem_block, pl.BlockSpec(memory_space=pl.ANY), row, vec, vec],
        out_specs=[row, row],
        scratch_shapes=[pltpu.VMEM((2, PAIRS, WORD_ROWS, LANES), U32),
                        pltpu.VMEM((WINDOW * WORD_ROWS, LANES), F32),
                        pltpu.VMEM((WINDOW * WORD_ROWS, LANES), F32),
                        pltpu.SemaphoreType.DMA((2,))],
    )
    return pl.pallas_call(
        _combine_kernel,
        grid_spec=grid_spec,
        out_shape=[jax.ShapeDtypeStruct((t, d), F32), jax.ShapeDtypeStruct((t, d), BF16)],
        compiler_params=_params(("arbitrary",)),
        name="combine_ln",
    )(cnt, start, off, pos, wts, rows_tiles, s, g.reshape(1, d), b.reshape(1, d))


def _row_layout(cnt):
    cnt2 = cnt.reshape(-1, N_EXPERTS)
    total = jnp.sum(cnt2, axis=0)
    padded = (total + EXPERT_TILE - 1) // EXPERT_TILE * EXPERT_TILE
    base = jnp.cumsum(padded) - padded
    off = base[None, :] + jnp.cumsum(cnt2, axis=0) - cnt2
    return (off.reshape(-1).astype(I32), (base + total).astype(I32), (padded - total).astype(I32),
            (base // EXPERT_TILE).astype(I32), (padded // EXPERT_TILE).astype(I32))


def _moe_block(x32, x16, s, layer, w_router, router_bias, w1, w3, w2, g, b):
    t, d = x32.shape
    nw = t // WINDOW
    n_tiles = t * TOP_K // EXPERT_TILE + N_EXPERTS
    n_rows = n_tiles * EXPERT_TILE
    wts, pos, cnt, start = _router_call(x32, w_router, router_bias)
    off, pad_start, pad_len, first_tile, tiles_of = _row_layout(cnt)
    cnt = cnt.reshape(-1)
    start = start.reshape(-1)
    pos = pos.reshape(nw, 1, PAIRS)
    wts = wts.reshape(nw, 1, PAIRS)
    rows = _dispatch_call(x16.reshape(t, ROW_TILES, LANES), pos, cnt, start, off, pad_start, pad_len, n_rows)
    out_rows = _expert_call(rows.reshape(n_rows * WORD_ROWS, LANES), first_tile, tiles_of, layer, w1, w3, w2)
    return _combine_call(out_rows.reshape(n_rows, WORD_ROWS, LANES), pos, wts, cnt, start, off, s, g, b)


def kernel(x, in_ln_g, in_ln_b, w_in, w_spatial, b_spatial, w_conv, w_pool, pool_scale,
           w_gate, b_gate, w_proj_a, w_proj_b, w_proj_c, w_out, ln1_g, ln1_b,
           w_router, router_bias, w_expert_gate, w_expert_up, w_expert_down,
           w_shared_gate, w_shared_up, w_shared_down, ln2_g, ln2_b):
    bsz, seq, d = x.shape
    x32, x16 = _ln_call(x.reshape(bsz * seq, d), in_ln_g, in_ln_b)
    for l in range(DEPTH):
        ya, yb, yc = _mix_tokens_call(x16, w_in[l].astype(BF16), w_spatial[l], b_spatial[l], w_conv[l],
                                      w_pool[l].astype(BF16), pool_scale[l])
        merged = _merge_call(x16, ya, yb, yc, w_gate[l].astype(BF16), b_gate[l],
                             w_proj_a[l].astype(BF16), w_proj_b[l].astype(BF16), w_proj_c[l].astype(BF16))
        x32, x16, s = _out_proj_call(merged, x32, w_out[l].astype(BF16), ln1_g[l], ln1_b[l],
                                     w_shared_gate[l].astype(BF16), w_shared_up[l].astype(BF16),
                                     w_shared_down[l].astype(BF16))
        x32, x16 = _moe_block(x32, x16, s, l, w_router[l], router_bias[l], w_expert_gate, w_expert_up,
                              w_expert_down, ln2_g[l], ln2_b[l])
    return x32.reshape(bsz, seq, d)
```

```python
import functools

import jax
import jax.numpy as jnp
from jax import lax
from jax.experimental import pallas as pl
from jax.experimental.pallas import tpu as pltpu

F32 = jnp.float32
BF16 = jnp.bfloat16
I32 = jnp.int32
U32 = jnp.uint32

D_MODEL = 2048
DEPTH = 2
CHUNK = 64
A_WIDTH = 1024
A_BLOCK = 128
N_HEAD_GROUPS = 4
HEAD = A_WIDTH // N_HEAD_GROUPS
N_STREAMS = 6
CONV_W = 3
N_BRANCH = 3
N_EXPERTS = 64
TOP_K = 8
N_GROUPS = 8
GROUP_SIZE = N_EXPERTS // N_GROUPS
TOPK_GROUPS = 4
ROUTED_SCALE = 2.5
ALPHA = (2.0 * DEPTH) ** 0.25
LN_EPS = 1e-5

LANES = 128
ROW_TILES = D_MODEL // LANES
WORD_ROWS = ROW_TILES // 2
MERGE_COLS = 512
WINDOW = 256
PAIRS = WINDOW * TOP_K
EXPERT_TILE = 512
VMEM_LIMIT = 56 * 1024 * 1024
HIGH_HALF = 0xFFFF0000

def _params(sem, vmem=VMEM_LIMIT):
    return pltpu.CompilerParams(dimension_semantics=sem, vmem_limit_bytes=vmem)


def _gelu(x):
    return 0.5 * x * (1.0 + lax.erf(x * (0.5 ** 0.5)))


def _layer_norm_rows(r, g, b):
    mu = jnp.mean(r, axis=-1, keepdims=True)
    c = r - mu
    var = jnp.mean(c * c, axis=-1, keepdims=True)
    return c * lax.rsqrt(var + LN_EPS) * g + b


def _unpack_words(words):
    low = pltpu.bitcast(lax.shift_left(words, jnp.uint32(16)), F32)
    high = pltpu.bitcast(words & jnp.uint32(HIGH_HALF), F32)
    return low, high


def _pack_words(low, high):
    lo = pltpu.bitcast(low.astype(BF16).astype(F32), U32)
    hi = pltpu.bitcast(high.astype(BF16).astype(F32), U32)
    return lax.shift_right_logical(lo, jnp.uint32(16)) | (hi & jnp.uint32(HIGH_HALF))


def _ln_kernel(x_ref, g_ref, b_ref, o32_ref, o16_ref):
    y = _layer_norm_rows(x_ref[...], g_ref[...], b_ref[...])
    o32_ref[...] = y
    o16_ref[...] = y.astype(BF16)


def _ln_call(x, g, b, tm=256):
    t, d = x.shape
    row = pl.BlockSpec((tm, d), lambda i: (i, 0))
    vec = pl.BlockSpec((1, d), lambda i: (0, 0))
    return pl.pallas_call(
        _ln_kernel,
        grid=(t // tm,),
        in_specs=[row, vec, vec],
        out_specs=[row, row],
        out_shape=[jax.ShapeDtypeStruct((t, d), F32), jax.ShapeDtypeStruct((t, d), BF16)],
        compiler_params=_params(("parallel",)),
        name="layer_norm",
    )(x, g.reshape(1, d), b.reshape(1, d))


def _mix_tokens_kernel(x_ref, wu_ref, wv_ref, wbg_ref, wcg_ref, wh_ref, wpp_ref,
                       ws_ref, bs_ref, wc_ref, wp_ref, ps_ref, ew_ref, ed_ref,
                       ya_ref, yb_ref, yc_ref, ew16_ref, ed16_ref, zc_ref, pc_ref):
    g = pl.program_id(0)
    i = pl.program_id(1)
    tm = x_ref.shape[0]
    ew16_ref[...] = ew_ref[0].astype(BF16)
    ed16_ref[...] = ed_ref[0].astype(BF16)

    @pl.when(i == 0)
    def _():
        zc_ref[...] = jnp.zeros_like(zc_ref)
        pc_ref[...] = jnp.zeros_like(pc_ref)

    x = x_ref[...]
    proj = lambda w_ref: jnp.dot(x, w_ref[...], preferred_element_type=F32)

    u = _gelu(proj(wu_ref))
    v = _gelu(proj(wv_ref))
    mu = jnp.mean(v, axis=-1, keepdims=True)
    vc = v - mu
    var = jnp.mean(vc * vc, axis=-1, keepdims=True)
    vn = (vc * lax.rsqrt(var + LN_EPS)).astype(BF16)
    qi = lax.broadcasted_iota(I32, (A_BLOCK, A_BLOCK), 0) // CHUNK
    kj = lax.broadcasted_iota(I32, (A_BLOCK, A_BLOCK), 1) // CHUNK
    ws = jnp.where(kj <= qi, ws_ref[0], 0.0).astype(BF16)
    bs = bs_ref[0]
    for n in range(tm // A_BLOCK):
        rows = slice(n * A_BLOCK, (n + 1) * A_BLOCK)
        mixed = jnp.dot(ws, vn[rows], preferred_element_type=F32) + bs
        ya_ref[rows, :] = (u[rows] * mixed).astype(BF16)

    z = proj(wcg_ref) * proj(wh_ref)
    row = lax.broadcasted_iota(I32, (tm, 1), 0)
    prev1 = zc_ref[7:8, :]
    prev2 = zc_ref[6:7, :]
    z1 = jnp.where(row == 0, prev1, pltpu.roll(z, 1, 0))
    z2 = jnp.where(row == 0, prev2, jnp.where(row == 1, prev1, pltpu.roll(z, 2, 0)))
    conv = wc_ref[0:1, :] * z2 + wc_ref[1:2, :] * z1 + wc_ref[2:3, :] * z
    yb_ref[...] = (proj(wbg_ref) * conv).astype(BF16)
    zc_ref[...] = z[tm - 8:, :]

    p = proj(wpp_ref)
    win = lax.shift_left(jnp.int32(2), g)
    p16 = p.astype(BF16)
    pext = jnp.concatenate([pc_ref[...], p16], axis=0)
    rr = lax.broadcasted_iota(I32, (tm, tm + A_BLOCK), 0) + A_BLOCK
    cc = lax.broadcasted_iota(I32, (tm, tm + A_BLOCK), 1)
    band = jnp.where((cc <= rr) & (cc > rr - win), 1.0, 0.0).astype(BF16)
    total = jnp.dot(band, pext, preferred_element_type=F32)
    count = jnp.minimum(i * tm + row + 1, win).astype(F32)
    pooled = (total / count - p).astype(BF16)
    yc = jnp.dot(pooled, wp_ref[0], preferred_element_type=F32) * ps_ref[...]
    yc_ref[...] = yc.astype(BF16)
    pc_ref[...] = p16[tm - A_BLOCK:, :]


def _cast_stream_specs(w, layer, n_steps, step_of, halves=1, half=0):
    _, ne, r, c = w.shape
    per = ne // n_steps
    assert per * n_steps == ne and r % halves == 0
    rows = r // halves
    return (pl.BlockSpec((1, per, rows, c), lambda *idx: (layer, step_of(*idx), half, 0)),
            pl.BlockSpec((per, rows, c), lambda *idx: (step_of(*idx), 0, 0)),
            jax.ShapeDtypeStruct((ne, rows, c), BF16))


def _mix_tokens_call(xb, w_in16, w_s, b_s, w_conv, w_pool16, pool_scale, layer, expert_w, expert_down, tm=512):
    t, d = xb.shape
    nt = t // tm
    step_of = lambda g, i, *_: g * nt + i
    ew_in, ew_out, ew_shape = _cast_stream_specs(expert_w, layer, N_HEAD_GROUPS * nt, step_of)
    ed_in, ed_out, ed_shape = _cast_stream_specs(expert_down, layer, N_HEAD_GROUPS * nt, step_of, 2, 0)
    stream = lambda s: pl.BlockSpec((d, HEAD), lambda g, i: (0, s * N_HEAD_GROUPS + g))
    grp = lambda *shape: pl.BlockSpec((1,) + shape, lambda g, i: (g,) + (0,) * len(shape))
    out_spec = pl.BlockSpec((tm, HEAD), lambda g, i: (i, g))
    out_shape = jax.ShapeDtypeStruct((t, A_WIDTH), BF16)
    return pl.pallas_call(
        _mix_tokens_kernel,
        grid=(N_HEAD_GROUPS, t // tm),
        in_specs=[pl.BlockSpec((tm, d), lambda g, i: (i, 0))]
        + [stream(s) for s in range(N_STREAMS)]
        + [grp(A_BLOCK, A_BLOCK), grp(A_BLOCK, 1),
           pl.BlockSpec((CONV_W, HEAD), lambda g, i: (0, g)),
           grp(HEAD, HEAD),
           pl.BlockSpec((1, HEAD), lambda g, i: (0, g)),
           ew_in, ed_in],
        out_specs=[out_spec, out_spec, out_spec, ew_out, ed_out],
        out_shape=[out_shape, out_shape, out_shape, ew_shape, ed_shape],
        scratch_shapes=[pltpu.VMEM((8, HEAD), F32), pltpu.VMEM((A_BLOCK, HEAD), BF16)],
        compiler_params=_params(("arbitrary", "arbitrary")),
        name="mix_tokens",
    )(xb, *([w_in16] * N_STREAMS), w_s, b_s.reshape(N_HEAD_GROUPS, A_BLOCK, 1), w_conv,
      w_pool16, pool_scale.reshape(1, A_WIDTH), expert_w, expert_down)


def _merge_kernel(x_ref, ya_ref, yb_ref, yc_ref, wg0_ref, wg1_ref, wg2_ref, bg0_ref, bg1_ref, bg2_ref,
                  wa_ref, wb_ref, wc_ref, ew_ref, ed_ref, m_ref, ew16_ref, ed16_ref):
    ew16_ref[...] = ew_ref[0].astype(BF16)
    ed16_ref[...] = ed_ref[0].astype(BF16)
    x = x_ref[...]
    gate = lambda w_ref, b_ref: jax.nn.sigmoid(
        jnp.dot(x, w_ref[...], preferred_element_type=F32) + b_ref[...])
    m = gate(wg0_ref, bg0_ref) * jnp.dot(ya_ref[...], wa_ref[...], preferred_element_type=F32)
    m = m + gate(wg1_ref, bg1_ref) * jnp.dot(yb_ref[...], wb_ref[...], preferred_element_type=F32)
    m = m + gate(wg2_ref, bg2_ref) * jnp.dot(yc_ref[...], wc_ref[...], preferred_element_type=F32)
    m_ref[...] = m.astype(BF16)


def _merge_call(xb, ya, yb, yc, w_gate16, b_gate, wa16, wb16, wc16, layer, expert_w, expert_down, tm=512):
    t, d = xb.shape
    nj = d // MERGE_COLS
    nt = t // tm
    step_of = lambda j, i, *_: j * nt + i
    ew_in, ew_out, ew_shape = _cast_stream_specs(expert_w, layer, nj * nt, step_of)
    ed_in, ed_out, ed_shape = _cast_stream_specs(expert_down, layer, nj * nt, step_of, 2, 1)
    tok = lambda width: pl.BlockSpec((tm, width), lambda j, i: (i, 0))
    gate_w = lambda br: pl.BlockSpec((d, MERGE_COLS), lambda j, i: (0, br * nj + j))
    gate_b = lambda br: pl.BlockSpec((1, MERGE_COLS), lambda j, i: (0, br * nj + j))
    branch_w = pl.BlockSpec((A_WIDTH, MERGE_COLS), lambda j, i: (0, j))
    b2 = b_gate.reshape(1, N_BRANCH * d)
    return pl.pallas_call(
        _merge_kernel,
        grid=(nj, t // tm),
        in_specs=[tok(d), tok(A_WIDTH), tok(A_WIDTH), tok(A_WIDTH)]
        + [gate_w(br) for br in range(N_BRANCH)] + [gate_b(br) for br in range(N_BRANCH)]
        + [branch_w] * N_BRANCH + [ew_in, ed_in],
        out_specs=[pl.BlockSpec((tm, MERGE_COLS), lambda j, i: (i, j)), ew_out, ed_out],
        out_shape=[jax.ShapeDtypeStruct((t, d), BF16), ew_shape, ed_shape],
        compiler_params=_params(("arbitrary", "arbitrary")),
        name="merge_branches",
    )(xb, ya, yb, yc, w_gate16, w_gate16, w_gate16, b2, b2, b2, wa16, wb16, wc16, expert_w, expert_down)


def _out_proj_kernel(m_ref, x_ref, w_ref, g_ref, b_ref, wg_ref, wu_ref, wd_ref, o32_ref, o16_ref, s_ref):
    r = ALPHA * x_ref[...] + jnp.dot(m_ref[...], w_ref[...], preferred_element_type=F32)
    y = _layer_norm_rows(r, g_ref[...], b_ref[...])
    yb = y.astype(BF16)
    o32_ref[...] = y
    o16_ref[...] = yb
    a = jnp.dot(yb, wg_ref[...], preferred_element_type=F32)
    b = jnp.dot(yb, wu_ref[...], preferred_element_type=F32)
    hid = (jax.nn.silu(a) * b).astype(BF16)
    s_ref[...] = ALPHA * y + jnp.dot(hid, wd_ref[...], preferred_element_type=F32)


def _out_proj_call(merged, x, w_out16, g, b, wg16, wu16, wd16, tm=512):
    t, d = x.shape
    hdim = wg16.shape[1]
    row = pl.BlockSpec((tm, d), lambda i: (i, 0))
    vec = pl.BlockSpec((1, d), lambda i: (0, 0))
    whole = lambda r, c: pl.BlockSpec((r, c), lambda i: (0, 0), pipeline_mode=pl.Buffered(1))
    return pl.pallas_call(
        _out_proj_kernel,
        grid=(t // tm,),
        in_specs=[row, row, whole(d, d), vec, vec, whole(d, hdim), whole(d, hdim), whole(hdim, d)],
        out_specs=[row, row, row],
        out_shape=[jax.ShapeDtypeStruct((t, d), F32), jax.ShapeDtypeStruct((t, d), BF16),
                   jax.ShapeDtypeStruct((t, d), F32)],
        compiler_params=_params(("parallel",)),
        name="out_proj_ln_shared",
    )(merged, x, w_out16, g.reshape(1, d), b.reshape(1, d), wg16, wu16, wd16)


def _dot_nt(a, b):
    return lax.dot_general(a, b, (((1,), (1,)), ((), ())), preferred_element_type=F32)


def _split_bf16(v):
    hi = v.astype(BF16)
    return hi, (v - hi.astype(F32)).astype(BF16)


def _to_groups(v):
    return jnp.stack([v[g * GROUP_SIZE:(g + 1) * GROUP_SIZE] for g in range(N_GROUPS)])


def _from_groups(v):
    return jnp.concatenate([v[g] for g in range(N_GROUPS)], axis=0)


def _over_experts(fn, v):
    return fn(fn(v, axis=0, keepdims=True), axis=1, keepdims=True)


def _router_kernel(x_ref, wt_ref, b_ref, wts_ref, pos_ref, cnt_ref, start_ref):
    tm = x_ref.shape[0]
    neg = -jnp.inf
    xh, xl = _split_bf16(x_ref[...])
    wh, wl = _split_bf16(wt_ref[...])
    logits = _dot_nt(wh, xh) + (_dot_nt(wh, xl) + _dot_nt(wl, xh))
    scores2d = jax.nn.sigmoid(logits)
    scores = _to_groups(scores2d)
    sel = _to_groups(scores2d + b_ref[...])
    shape3 = (N_GROUPS, GROUP_SIZE, tm)
    member = lax.broadcasted_iota(I32, shape3, 1)
    expert = lax.broadcasted_iota(I32, shape3, 0) * GROUP_SIZE + member

    m1 = jnp.max(sel, axis=1, keepdims=True)
    i1 = jnp.min(jnp.where(sel == m1, member, GROUP_SIZE), axis=1, keepdims=True)
    m2 = jnp.max(jnp.where(member == i1, neg, sel), axis=1, keepdims=True)
    gscore = m1 + m2

    gid = lax.broadcasted_iota(I32, (N_GROUPS, 1, tm), 0)
    keep = jnp.zeros((N_GROUPS, 1, tm), jnp.bool_)
    for _ in range(TOPK_GROUPS):
        m = jnp.max(gscore, axis=0, keepdims=True)
        gsel = jnp.min(jnp.where(gscore == m, gid, N_GROUPS), axis=0, keepdims=True)
        hit = gid == gsel
        keep = keep | hit
        gscore = jnp.where(hit, neg, gscore)

    cand = jnp.where(keep, sel, neg)
    chosen, weight = [], []
    for _ in range(TOP_K):
        m = _over_experts(jnp.max, cand)
        idx = _over_experts(jnp.min, jnp.where(cand == m, expert, N_EXPERTS))
        hit = expert == idx
        chosen.append(idx)
        weight.append(_over_experts(jnp.sum, jnp.where(hit, scores, 0.0))[0])
        cand = jnp.where(hit, neg, cand)
    norm = functools.reduce(jnp.add, weight)
    wts_ref[0] = jnp.concatenate(weight, axis=0) / norm * ROUTED_SCALE

    onehot = functools.reduce(jnp.add, [jnp.where(expert == idx, 1.0, 0.0) for idx in chosen])
    onehot = _from_groups(onehot)
    t_r = lax.broadcasted_iota(I32, (tm, tm), 0)
    t_c = lax.broadcasted_iota(I32, (tm, tm), 1)
    earlier = jnp.where(t_r < t_c, 1.0, 0.0).astype(BF16)
    rank = jnp.dot(onehot.astype(BF16), earlier, preferred_element_type=F32)
    count = jnp.sum(onehot, axis=1, keepdims=True)
    e_r = lax.broadcasted_iota(I32, (N_EXPERTS, N_EXPERTS), 0)
    e_c = lax.broadcasted_iota(I32, (N_EXPERTS, N_EXPERTS), 1)
    before = jnp.where(e_c < e_r, 1.0, 0.0).astype(BF16)
    c_hi, c_lo = _split_bf16(jnp.broadcast_to(count, (N_EXPERTS, tm)))
    start = (jnp.dot(before, c_hi, preferred_element_type=F32)
             + jnp.dot(before, c_lo, preferred_element_type=F32))
    where_to = _to_groups(rank + start)
    pos = [_over_experts(jnp.sum, jnp.where(expert == idx, where_to, 0.0))[0] for idx in chosen]
    pos_ref[0] = jnp.concatenate(pos, axis=0).astype(I32)
    cnt_ref[0] = count.astype(I32)
    start_ref[0] = start[:, 0:1].astype(I32)


def _router_call(x, w_router, router_bias):
    t, d = x.shape
    nw = t // WINDOW
    per_tok = pl.BlockSpec((1, TOP_K, WINDOW), lambda i: (i, 0, 0))
    per_win = pl.BlockSpec((1, N_EXPERTS, 1), lambda i: (i, 0, 0))
    return pl.pallas_call(
        _router_kernel,
        grid=(nw,),
        in_specs=[pl.BlockSpec((WINDOW, d), lambda i: (i, 0)),
                  pl.BlockSpec((N_EXPERTS, d), lambda i: (0, 0)),
                  pl.BlockSpec((N_EXPERTS, 1), lambda i: (0, 0))],
        out_specs=[per_tok, per_tok, per_win, per_win],
        out_shape=[jax.ShapeDtypeStruct((nw, TOP_K, WINDOW), F32),
                   jax.ShapeDtypeStruct((nw, TOP_K, WINDOW), I32),
                   jax.ShapeDtypeStruct((nw, N_EXPERTS, 1), I32),
                   jax.ShapeDtypeStruct((nw, N_EXPERTS, 1), I32)],
        compiler_params=_params(("parallel",)),
        name="router",
    )(x, w_router.T, router_bias.reshape(N_EXPERTS, 1))


def _segment_copies(cnt_ref, start_ref, off_ref, win, local, remote, sem, to_remote, act):
    def body(e, carry):
        n = cnt_ref[win * N_EXPERTS + e]

        @pl.when(n > 0)
        def _():
            loc = local.at[pl.ds(start_ref[win * N_EXPERTS + e], n)]
            rem = remote.at[pl.ds(off_ref[win * N_EXPERTS + e], n)]
            act(pltpu.make_async_copy(loc, rem, sem) if to_remote
                else pltpu.make_async_copy(rem, loc, sem))
        return carry
    lax.fori_loop(0, N_EXPERTS, body, 0)


def _pad_copies(pad_start_ref, pad_len_ref, zeros, remote, sem, act):
    def body(e, carry):
        n = pad_len_ref[e]

        @pl.when(n > 0)
        def _():
            act(pltpu.make_async_copy(zeros.at[pl.ds(0, n)], remote.at[pl.ds(pad_start_ref[e], n)], sem))
        return carry
    lax.fori_loop(0, N_EXPERTS, body, 0)


def _dispatch_kernel(cnt_ref, start_ref, off_ref, pad_start_ref, pad_len_ref,
                     pos_ref, x_ref, rows_hbm, buf, zeros, sem, pad_sem):
    w = pl.program_id(0)
    last = pl.num_programs(0) - 1
    slot = w % 2
    start = lambda c: c.start()
    wait = lambda c: c.wait()
    copies = functools.partial(_segment_copies, cnt_ref, start_ref, off_ref,
                               remote=rows_hbm, to_remote=True)

    @pl.when(w == 0)
    def _():
        zeros[...] = jnp.zeros_like(zeros)
        _pad_copies(pad_start_ref, pad_len_ref, zeros, rows_hbm, pad_sem, start)

    @pl.when(w >= 2)
    def _():
        copies(win=w - 2, local=buf.at[slot], sem=sem.at[slot], act=wait)

    def place(t, carry):
        v = pltpu.bitcast(x_ref[t], U32)
        for k in range(TOP_K):
            buf[slot, pos_ref[0, 0, k * WINDOW + t]] = v
        return carry
    lax.fori_loop(0, WINDOW, place, 0, unroll=4)
    copies(win=w, local=buf.at[slot], sem=sem.at[slot], act=start)

    @pl.when(w == last)
    def _():
        @pl.when(w >= 1)
        def _():
            copies(win=w - 1, local=buf.at[1 - slot], sem=sem.at[1 - slot], act=wait)
        copies(win=w, local=buf.at[slot], sem=sem.at[slot], act=wait)
        _pad_copies(pad_start_ref, pad_len_ref, zeros, rows_hbm, pad_sem, wait)


def _dispatch_call(x_tiles, pos, cnt, start, off, pad_start, pad_len, n_rows):
    t = x_tiles.shape[0]
    nw = t // WINDOW
    grid_spec = pltpu.PrefetchScalarGridSpec(
        num_scalar_prefetch=5,
        grid=(nw,),
        in_specs=[pl.BlockSpec((1, 1, PAIRS), lambda w, *_: (w, 0, 0), memory_space=pltpu.SMEM),
                  pl.BlockSpec((WINDOW, ROW_TILES, LANES), lambda w, *_: (w, 0, 0))],
        out_specs=pl.BlockSpec(memory_space=pl.ANY),
        scratch_shapes=[pltpu.VMEM((2, PAIRS, WORD_ROWS, LANES), U32),
                        pltpu.VMEM((EXPERT_TILE, WORD_ROWS, LANES), U32),
                        pltpu.SemaphoreType.DMA((2,)),
                        pltpu.SemaphoreType.DMA(())],
    )
    return pl.pallas_call(
        _dispatch_kernel,
        grid_spec=grid_spec,
        out_shape=jax.ShapeDtypeStruct((n_rows, WORD_ROWS, LANES), U32),
        compiler_params=_params(("arbitrary",)),
        name="dispatch",
    )(cnt, start, off, pad_start, pad_len, pos, x_tiles)


def _expert_kernel(first_tile_ref, n_tiles_ref, rows_hbm, w1_ref, w3_ref, w2a_ref, w2b_ref, out_hbm,
                   xin, yout, xt, in_sem, out_sem):
    e = pl.program_id(0)
    last = pl.num_programs(0) - 1
    tr = xt.shape[0]
    blk = tr * WORD_ROWS
    g0 = first_tile_ref[e]
    n = n_tiles_ref[e]
    used = first_tile_ref[last] + n_tiles_ref[last]

    def tile_rows(g):
        return pl.ds(pl.multiple_of(g * blk, blk), blk)

    def in_copy(g, slot):
        return pltpu.make_async_copy(rows_hbm.at[tile_rows(g)], xin.at[slot], in_sem.at[slot])

    def out_copy(g, slot):
        return pltpu.make_async_copy(yout.at[slot], out_hbm.at[tile_rows(g)], out_sem.at[slot])

    @pl.when(e == 0)
    def _():
        in_copy(0, 0).start()

    def tile(j, carry):
        g = g0 + j
        slot = g % 2
        in_copy(g, slot).wait()

        @pl.when(g + 1 < used)
        def _():
            in_copy(g + 1, 1 - slot).start()

        @pl.when(g >= 2)
        def _():
            out_copy(g - 2, slot).wait()

        for s in range(WORD_ROWS):
            low, high = _unpack_words(xin[slot, pl.ds(s, tr, stride=WORD_ROWS), :])
            xt[:, (2 * s) * LANES:(2 * s + 1) * LANES] = low.astype(BF16)
            xt[:, (2 * s + 1) * LANES:(2 * s + 2) * LANES] = high.astype(BF16)
        x = xt[...]
        a = jnp.dot(x, w1_ref[0], preferred_element_type=F32)
        b = jnp.dot(x, w3_ref[0], preferred_element_type=F32)
        hid = (jax.nn.silu(a) * b).astype(BF16)
        half = w2a_ref.shape[1]
        y = (jnp.dot(hid[:, :half], w2a_ref[0], preferred_element_type=F32)
             + jnp.dot(hid[:, half:], w2b_ref[0], preferred_element_type=F32))
        for s in range(WORD_ROWS):
            yout[slot, pl.ds(s, tr, stride=WORD_ROWS), :] = _pack_words(
                y[:, (2 * s) * LANES:(2 * s + 1) * LANES], y[:, (2 * s + 1) * LANES:(2 * s + 2) * LANES])
        out_copy(g, slot).start()
        return carry
    lax.fori_loop(0, n, tile, 0)

    @pl.when(e == last)
    def _():
        @pl.when(used >= 2)
        def _():
            out_copy(used - 2, used % 2).wait()
        out_copy(used - 1, (used - 1) % 2).wait()


def _expert_call(rows2d, first_tile, n_tiles, w1, w3, w2a, w2b):
    ne, d, hdim = w1.shape
    per_expert = lambda w: pl.BlockSpec((1,) + w.shape[1:], lambda e, *_: (e, 0, 0))
    blk = EXPERT_TILE * WORD_ROWS
    grid_spec = pltpu.PrefetchScalarGridSpec(
        num_scalar_prefetch=2,
        grid=(ne,),
        in_specs=[pl.BlockSpec(memory_space=pl.ANY),
                  per_expert(w1), per_expert(w3), per_expert(w2a), per_expert(w2b)],
        out_specs=pl.BlockSpec(memory_space=pl.ANY),
        scratch_shapes=[pltpu.VMEM((2, blk, LANES), U32), pltpu.VMEM((2, blk, LANES), U32),
                        pltpu.VMEM((EXPERT_TILE, d), BF16),
                        pltpu.SemaphoreType.DMA((2,)), pltpu.SemaphoreType.DMA((2,))],
    )
    return pl.pallas_call(
        _expert_kernel,
        grid_spec=grid_spec,
        out_shape=jax.ShapeDtypeStruct(rows2d.shape, U32),
        compiler_params=_params(("arbitrary",)),
        name="routed_experts",
    )(first_tile, n_tiles, rows2d, w1, w3, w2a, w2b)


def _combine_kernel(cnt_ref, start_ref, off_ref, pos_ref, wts_ref, rows_hbm, s_ref, g_ref, b_ref,
                    o32_ref, o16_ref, buf, acc_lo, acc_hi, sem):
    w = pl.program_id(0)
    last = pl.num_programs(0) - 1
    slot = w % 2
    start = lambda c: c.start()
    wait = lambda c: c.wait()
    copies = functools.partial(_segment_copies, cnt_ref, start_ref, off_ref,
                               remote=rows_hbm, to_remote=False)

    @pl.when(w == 0)
    def _():
        copies(win=w, local=buf.at[slot], sem=sem.at[slot], act=start)

    @pl.when(w < last)
    def _():
        copies(win=w + 1, local=buf.at[1 - slot], sem=sem.at[1 - slot], act=start)

    copies(win=w, local=buf.at[slot], sem=sem.at[slot], act=wait)

    def token(t, carry):
        lo, hi = _unpack_words(buf[slot, pos_ref[0, 0, t]])
        wt = wts_ref[0, 0, t]
        lo, hi = wt * lo, wt * hi
        for k in range(1, TOP_K):
            l_k, h_k = _unpack_words(buf[slot, pos_ref[0, 0, k * WINDOW + t]])
            wt = wts_ref[0, 0, k * WINDOW + t]
            lo, hi = lo + wt * l_k, hi + wt * h_k
        at = pl.ds(pl.multiple_of(t * WORD_ROWS, WORD_ROWS), WORD_ROWS)
        acc_lo[at, :] = lo
        acc_hi[at, :] = hi
        return carry
    lax.fori_loop(0, WINDOW, token, 0, unroll=2)

    chunks = []
    for s in range(WORD_ROWS):
        chunks.append(acc_lo[pl.ds(s, WINDOW, stride=WORD_ROWS), :])
        chunks.append(acc_hi[pl.ds(s, WINDOW, stride=WORD_ROWS), :])
    out = _layer_norm_rows(s_ref[...] + jnp.concatenate(chunks, axis=1), g_ref[...], b_ref[...])
    o32_ref[...] = out
    o16_ref[...] = out.astype(BF16)


def _combine_call(rows_tiles, pos, wts, cnt, start, off, s, g, b):
    t, d = s.shape
    nw = t // WINDOW
    smem_block = pl.BlockSpec((1, 1, PAIRS), lambda w, *_: (w, 0, 0), memory_space=pltpu.SMEM)
    row = pl.BlockSpec((WINDOW, d), lambda w, *_: (w, 0))
    vec = pl.BlockSpec((1, d), lambda w, *_: (0, 0))
    grid_spec = pltpu.PrefetchScalarGridSpec(
        num_scalar_prefetch=3,
        grid=(nw,),
        in_specs=[smem_block, smem_block, pl.BlockSpec(memory_space=pl.ANY), row, vec, vec],
        out_specs=[row, row],
        scratch_shapes=[pltpu.VMEM((2, PAIRS, WORD_ROWS, LANES), U32),
                        pltpu.VMEM((WINDOW * WORD_ROWS, LANES), F32),
                        pltpu.VMEM((WINDOW * WORD_ROWS, LANES), F32),
                        pltpu.SemaphoreType.DMA((2,))],
    )
    return pl.pallas_call(
        _combine_kernel,
        grid_spec=grid_spec,
        out_shape=[jax.ShapeDtypeStruct((t, d), F32), jax.ShapeDtypeStruct((t, d), BF16)],
        compiler_params=_params(("arbitrary",)),
        name="combine_ln",
    )(cnt, start, off, pos, wts, rows_tiles, s, g.reshape(1, d), b.reshape(1, d))


def _row_layout(cnt):
    cnt2 = cnt.reshape(-1, N_EXPERTS)
    total = jnp.sum(cnt2, axis=0)
    padded = (total + EXPERT_TILE - 1) // EXPERT_TILE * EXPERT_TILE
    base = jnp.cumsum(padded) - padded
    off = base[None, :] + jnp.cumsum(cnt2, axis=0) - cnt2
    return (off.reshape(-1).astype(I32), (base + total).astype(I32), (padded - total).astype(I32),
            (base // EXPERT_TILE).astype(I32), (padded // EXPERT_TILE).astype(I32))


def _moe_block(x32, x16, s, w_router, router_bias, w1_16, w3_16, w2a_16, w2b_16, g, b):
    t, d = x32.shape
    nw = t // WINDOW
    n_tiles = t * TOP_K // EXPERT_TILE + N_EXPERTS
    n_rows = n_tiles * EXPERT_TILE
    wts, pos, cnt, start = _router_call(x32, w_router, router_bias)
    off, pad_start, pad_len, first_tile, tiles_of = _row_layout(cnt)
    cnt = cnt.reshape(-1)
    start = start.reshape(-1)
    pos = pos.reshape(nw, 1, PAIRS)
    wts = wts.reshape(nw, 1, PAIRS)
    rows = _dispatch_call(x16.reshape(t, ROW_TILES, LANES), pos, cnt, start, off, pad_start, pad_len, n_rows)
    out_rows = _expert_call(rows.reshape(n_rows * WORD_ROWS, LANES), first_tile, tiles_of,
                            w1_16, w3_16, w2a_16, w2b_16)
    return _combine_call(out_rows.reshape(n_rows, WORD_ROWS, LANES), pos, wts, cnt, start, off, s, g, b)


def kernel(x, in_ln_g, in_ln_b, w_in, w_spatial, b_spatial, w_conv, w_pool, pool_scale,
           w_gate, b_gate, w_proj_a, w_proj_b, w_proj_c, w_out, ln1_g, ln1_b,
           w_router, router_bias, w_expert_gate, w_expert_up, w_expert_down,
           w_shared_gate, w_shared_up, w_shared_down, ln2_g, ln2_b):
    bsz, seq, d = x.shape
    x32, x16 = _ln_call(x.reshape(bsz * seq, d), in_ln_g, in_ln_b)
    for l in range(DEPTH):
        ya, yb, yc, w1_16, w2a_16 = _mix_tokens_call(
            x16, w_in[l].astype(BF16), w_spatial[l], b_spatial[l], w_conv[l], w_pool[l].astype(BF16),
            pool_scale[l], l, w_expert_gate, w_expert_down)
        merged, w3_16, w2b_16 = _merge_call(
            x16, ya, yb, yc, w_gate[l].astype(BF16), b_gate[l], w_proj_a[l].astype(BF16),
            w_proj_b[l].astype(BF16), w_proj_c[l].astype(BF16), l, w_expert_up, w_expert_down)
        x32, x16, s = _out_proj_call(merged, x32, w_out[l].astype(BF16), ln1_g[l], ln1_b[l],
                                     w_shared_gate[l].astype(BF16), w_shared_up[l].astype(BF16),
                                     w_shared_down[l].astype(BF16))
        x32, x16 = _moe_block(x32, x16, s, w_router[l], router_bias[l], w1_16, w3_16, w2a_16, w2b_16,
                              ln2_g[l], ln2_b[l])
    return x32.reshape(bsz, seq, d)
```

```python
import functools

import jax
import jax.numpy as jnp
from jax import lax
from jax.experimental import pallas as pl
from jax.experimental.pallas import tpu as pltpu

F32 = jnp.float32
BF16 = jnp.bfloat16
I32 = jnp.int32

D_MODEL = 2048
DEPTH = 2
CHUNK = 64
A_WIDTH = 1024
A_BLOCK = 128
N_HEAD_GROUPS = 4
HEAD = A_WIDTH // N_HEAD_GROUPS
N_STREAMS = 6
CONV_W = 3
N_BRANCH = 3
N_EXPERTS = 64
TOP_K = 8
N_GROUPS = 8
GROUP_SIZE = N_EXPERTS // N_GROUPS
TOPK_GROUPS = 4
ROUTED_SCALE = 2.5
ALPHA = (2.0 * DEPTH) ** 0.25
LN_EPS = 1e-5

LANES = 128
ROW_TILES = D_MODEL // LANES
MERGE_COLS = 512
WINDOW = 256
PAIRS = WINDOW * TOP_K
EXPERT_TILE = 512
EXPERT_CHUNK = 128
VMEM_LIMIT = 56 * 1024 * 1024

def _params(sem, vmem=VMEM_LIMIT):
    return pltpu.CompilerParams(dimension_semantics=sem, vmem_limit_bytes=vmem)


def _gelu(x):
    return 0.5 * x * (1.0 + lax.erf(x * (0.5 ** 0.5)))


def _layer_norm_rows(r, g, b):
    mu = jnp.mean(r, axis=-1, keepdims=True)
    c = r - mu
    var = jnp.mean(c * c, axis=-1, keepdims=True)
    return c * lax.rsqrt(var + LN_EPS) * g + b


def _ln_kernel(x_ref, g_ref, b_ref, o32_ref, o16_ref):
    y = _layer_norm_rows(x_ref[...], g_ref[...], b_ref[...])
    o32_ref[...] = y
    o16_ref[...] = y.astype(BF16)


def _ln_call(x, g, b, tm=256):
    t, d = x.shape
    row = pl.BlockSpec((tm, d), lambda i: (i, 0))
    vec = pl.BlockSpec((1, d), lambda i: (0, 0))
    return pl.pallas_call(
        _ln_kernel,
        grid=(t // tm,),
        in_specs=[row, vec, vec],
        out_specs=[row, row],
        out_shape=[jax.ShapeDtypeStruct((t, d), F32), jax.ShapeDtypeStruct((t, d), BF16)],
        compiler_params=_params(("parallel",)),
        name="layer_norm",
    )(x, g.reshape(1, d), b.reshape(1, d))


def _mix_tokens_kernel(x_ref, wu_ref, wv_ref, wbg_ref, wcg_ref, wh_ref, wpp_ref,
                       ws_ref, bs_ref, wc_ref, wp_ref, ps_ref, ew_ref, ed_ref,
                       ya_ref, yb_ref, yc_ref, ew16_ref, ed16_ref, zc_ref, pc_ref):
    g = pl.program_id(0)
    i = pl.program_id(1)
    tm = x_ref.shape[0]
    ew16_ref[...] = ew_ref[0].astype(BF16)
    ed16_ref[...] = ed_ref[0].astype(BF16)

    @pl.when(i == 0)
    def _():
        zc_ref[...] = jnp.zeros_like(zc_ref)
        pc_ref[...] = jnp.zeros_like(pc_ref)

    x = x_ref[...]
    proj = lambda w_ref: jnp.dot(x, w_ref[...], preferred_element_type=F32)

    u = _gelu(proj(wu_ref))
    v = _gelu(proj(wv_ref))
    mu = jnp.mean(v, axis=-1, keepdims=True)
    vc = v - mu
    var = jnp.mean(vc * vc, axis=-1, keepdims=True)
    vn = (vc * lax.rsqrt(var + LN_EPS)).astype(BF16)
    qi = lax.broadcasted_iota(I32, (A_BLOCK, A_BLOCK), 0) // CHUNK
    kj = lax.broadcasted_iota(I32, (A_BLOCK, A_BLOCK), 1) // CHUNK
    ws = jnp.where(kj <= qi, ws_ref[0], 0.0).astype(BF16)
    bs = bs_ref[0]
    for n in range(tm // A_BLOCK):
        rows = slice(n * A_BLOCK, (n + 1) * A_BLOCK)
        mixed = jnp.dot(ws, vn[rows], preferred_element_type=F32) + bs
        ya_ref[rows, :] = (u[rows] * mixed).astype(BF16)

    z = proj(wcg_ref) * proj(wh_ref)
    row = lax.broadcasted_iota(I32, (tm, 1), 0)
    prev1 = zc_ref[7:8, :]
    prev2 = zc_ref[6:7, :]
    z1 = jnp.where(row == 0, prev1, pltpu.roll(z, 1, 0))
    z2 = jnp.where(row == 0, prev2, jnp.where(row == 1, prev1, pltpu.roll(z, 2, 0)))
    conv = wc_ref[0:1, :] * z2 + wc_ref[1:2, :] * z1 + wc_ref[2:3, :] * z
    yb_ref[...] = (proj(wbg_ref) * conv).astype(BF16)
    zc_ref[...] = z[tm - 8:, :]

    p = proj(wpp_ref)
    win = lax.shift_left(jnp.int32(2), g)
    p16 = p.astype(BF16)
    pext = jnp.concatenate([pc_ref[...], p16], axis=0)
    rr = lax.broadcasted_iota(I32, (tm, tm + A_BLOCK), 0) + A_BLOCK
    cc = lax.broadcasted_iota(I32, (tm, tm + A_BLOCK), 1)
    band = jnp.where((cc <= rr) & (cc > rr - win), 1.0, 0.0).astype(BF16)
    total = jnp.dot(band, pext, preferred_element_type=F32)
    count = jnp.minimum(i * tm + row + 1, win).astype(F32)
    pooled = (total / count - p).astype(BF16)
    yc = jnp.dot(pooled, wp_ref[0], preferred_element_type=F32) * ps_ref[...]
    yc_ref[...] = yc.astype(BF16)
    pc_ref[...] = p16[tm - A_BLOCK:, :]


def _cast_stream_specs(w, layer, n_steps, step_of, halves=1, half=0):
    _, ne, r, c = w.shape
    per = ne // n_steps
    assert per * n_steps == ne and r % halves == 0
    rows = r // halves
    return (pl.BlockSpec((1, per, rows, c), lambda *idx: (layer, step_of(*idx), half, 0)),
            pl.BlockSpec((per, rows, c), lambda *idx: (step_of(*idx), 0, 0)),
            jax.ShapeDtypeStruct((ne, rows, c), BF16))


def _mix_tokens_call(xb, w_in16, w_s, b_s, w_conv, w_pool16, pool_scale, layer, expert_w, expert_down, tm=512):
    t, d = xb.shape
    nt = t // tm
    step_of = lambda g, i, *_: g * nt + i
    ew_in, ew_out, ew_shape = _cast_stream_specs(expert_w, layer, N_HEAD_GROUPS * nt, step_of)
    ed_in, ed_out, ed_shape = _cast_stream_specs(expert_down, layer, N_HEAD_GROUPS * nt, step_of, 2, 0)
    stream = lambda s: pl.BlockSpec((d, HEAD), lambda g, i: (0, s * N_HEAD_GROUPS + g))
    grp = lambda *shape: pl.BlockSpec((1,) + shape, lambda g, i: (g,) + (0,) * len(shape))
    out_spec = pl.BlockSpec((tm, HEAD), lambda g, i: (i, g))
    out_shape = jax.ShapeDtypeStruct((t, A_WIDTH), BF16)
    return pl.pallas_call(
        _mix_tokens_kernel,
        grid=(N_HEAD_GROUPS, t // tm),
        in_specs=[pl.BlockSpec((tm, d), lambda g, i: (i, 0))]
        + [stream(s) for s in range(N_STREAMS)]
        + [grp(A_BLOCK, A_BLOCK), grp(A_BLOCK, 1),
           pl.BlockSpec((CONV_W, HEAD), lambda g, i: (0, g)),
           grp(HEAD, HEAD),
           pl.BlockSpec((1, HEAD), lambda g, i: (0, g)),
           ew_in, ed_in],
        out_specs=[out_spec, out_spec, out_spec, ew_out, ed_out],
        out_shape=[out_shape, out_shape, out_shape, ew_shape, ed_shape],
        scratch_shapes=[pltpu.VMEM((8, HEAD), F32), pltpu.VMEM((A_BLOCK, HEAD), BF16)],
        compiler_params=_params(("arbitrary", "arbitrary")),
        name="mix_tokens",
    )(xb, *([w_in16] * N_STREAMS), w_s, b_s.reshape(N_HEAD_GROUPS, A_BLOCK, 1), w_conv,
      w_pool16, pool_scale.reshape(1, A_WIDTH), expert_w, expert_down)


def _merge_kernel(x_ref, ya_ref, yb_ref, yc_ref, wg0_ref, wg1_ref, wg2_ref, bg0_ref, bg1_ref, bg2_ref,
                  wa_ref, wb_ref, wc_ref, ew_ref, ed_ref, m_ref, ew16_ref, ed16_ref):
    ew16_ref[...] = ew_ref[0].astype(BF16)
    ed16_ref[...] = ed_ref[0].astype(BF16)
    x = x_ref[...]
    gate = lambda w_ref, b_ref: jax.nn.sigmoid(
        jnp.dot(x, w_ref[...], preferred_element_type=F32) + b_ref[...])
    m = gate(wg0_ref, bg0_ref) * jnp.dot(ya_ref[...], wa_ref[...], preferred_element_type=F32)
    m = m + gate(wg1_ref, bg1_ref) * jnp.dot(yb_ref[...], wb_ref[...], preferred_element_type=F32)
    m = m + gate(wg2_ref, bg2_ref) * jnp.dot(yc_ref[...], wc_ref[...], preferred_element_type=F32)
    m_ref[...] = m.astype(BF16)


def _merge_call(xb, ya, yb, yc, w_gate16, b_gate, wa16, wb16, wc16, layer, expert_w, expert_down, tm=512):
    t, d = xb.shape
    nj = d // MERGE_COLS
    nt = t // tm
    step_of = lambda j, i, *_: j * nt + i
    ew_in, ew_out, ew_shape = _cast_stream_specs(expert_w, layer, nj * nt, step_of)
    ed_in, ed_out, ed_shape = _cast_stream_specs(expert_down, layer, nj * nt, step_of, 2, 1)
    tok = lambda width: pl.BlockSpec((tm, width), lambda j, i: (i, 0))
    gate_w = lambda br: pl.BlockSpec((d, MERGE_COLS), lambda j, i: (0, br * nj + j))
    gate_b = lambda br: pl.BlockSpec((1, MERGE_COLS), lambda j, i: (0, br * nj + j))
    branch_w = pl.BlockSpec((A_WIDTH, MERGE_COLS), lambda j, i: (0, j))
    b2 = b_gate.reshape(1, N_BRANCH * d)
    return pl.pallas_call(
        _merge_kernel,
        grid=(nj, t // tm),
        in_specs=[tok(d), tok(A_WIDTH), tok(A_WIDTH), tok(A_WIDTH)]
        + [gate_w(br) for br in range(N_BRANCH)] + [gate_b(br) for br in range(N_BRANCH)]
        + [branch_w] * N_BRANCH + [ew_in, ed_in],
        out_specs=[pl.BlockSpec((tm, MERGE_COLS), lambda j, i: (i, j)), ew_out, ed_out],
        out_shape=[jax.ShapeDtypeStruct((t, d), BF16), ew_shape, ed_shape],
        compiler_params=_params(("arbitrary", "arbitrary")),
        name="merge_branches",
    )(xb, ya, yb, yc, w_gate16, w_gate16, w_gate16, b2, b2, b2, wa16, wb16, wc16, expert_w, expert_down)


def _out_proj_kernel(m_ref, x_ref, w_ref, g_ref, b_ref, wg_ref, wu_ref, wd_ref, o32_ref, o16_ref, s_ref):
    r = ALPHA * x_ref[...] + jnp.dot(m_ref[...], w_ref[...], preferred_element_type=F32)
    y = _layer_norm_rows(r, g_ref[...], b_ref[...])
    yb = y.astype(BF16)
    o32_ref[...] = y
    o16_ref[...] = yb
    a = jnp.dot(yb, wg_ref[...], preferred_element_type=F32)
    b = jnp.dot(yb, wu_ref[...], preferred_element_type=F32)
    hid = (jax.nn.silu(a) * b).astype(BF16)
    s_ref[...] = ALPHA * y + jnp.dot(hid, wd_ref[...], preferred_element_type=F32)


def _out_proj_call(merged, x, w_out16, g, b, wg16, wu16, wd16, tm=512):
    t, d = x.shape
    hdim = wg16.shape[1]
    row = pl.BlockSpec((tm, d), lambda i: (i, 0))
    vec = pl.BlockSpec((1, d), lambda i: (0, 0))
    whole = lambda r, c: pl.BlockSpec((r, c), lambda i: (0, 0), pipeline_mode=pl.Buffered(1))
    return pl.pallas_call(
        _out_proj_kernel,
        grid=(t // tm,),
        in_specs=[row, row, whole(d, d), vec, vec, whole(d, hdim), whole(d, hdim), whole(hdim, d)],
        out_specs=[row, row, row],
        out_shape=[jax.ShapeDtypeStruct((t, d), F32), jax.ShapeDtypeStruct((t, d), BF16),
                   jax.ShapeDtypeStruct((t, d), F32)],
        compiler_params=_params(("parallel",)),
        name="out_proj_ln_shared",
    )(merged, x, w_out16, g.reshape(1, d), b.reshape(1, d), wg16, wu16, wd16)


def _dot_nt(a, b):
    return lax.dot_general(a, b, (((1,), (1,)), ((), ())), preferred_element_type=F32)


def _split_bf16(v):
    hi = v.astype(BF16)
    return hi, (v - hi.astype(F32)).astype(BF16)


def _to_groups(v):
    return jnp.stack([v[g * GROUP_SIZE:(g + 1) * GROUP_SIZE] for g in range(N_GROUPS)])


def _from_groups(v):
    return jnp.concatenate([v[g] for g in range(N_GROUPS)], axis=0)


def _over_experts(fn, v):
    return fn(fn(v, axis=0, keepdims=True), axis=1, keepdims=True)


def _router_kernel(x_ref, wt_ref, b_ref, wts_ref, pos_ref, cnt_ref, start_ref):
    tm = x_ref.shape[0]
    neg = -jnp.inf
    xh, xl = _split_bf16(x_ref[...])
    wh, wl = _split_bf16(wt_ref[...])
    logits = _dot_nt(wh, xh) + (_dot_nt(wh, xl) + _dot_nt(wl, xh))
    scores2d = jax.nn.sigmoid(logits)
    scores = _to_groups(scores2d)
    sel = _to_groups(scores2d + b_ref[...])
    shape3 = (N_GROUPS, GROUP_SIZE, tm)
    member = lax.broadcasted_iota(I32, shape3, 1)
    expert = lax.broadcasted_iota(I32, shape3, 0) * GROUP_SIZE + member

    m1 = jnp.max(sel, axis=1, keepdims=True)
    i1 = jnp.min(jnp.where(sel == m1, member, GROUP_SIZE), axis=1, keepdims=True)
    m2 = jnp.max(jnp.where(member == i1, neg, sel), axis=1, keepdims=True)
    gscore = m1 + m2

    gid = lax.broadcasted_iota(I32, (N_GROUPS, 1, tm), 0)
    keep = jnp.zeros((N_GROUPS, 1, tm), jnp.bool_)
    for _ in range(TOPK_GROUPS):
        m = jnp.max(gscore, axis=0, keepdims=True)
        gsel = jnp.min(jnp.where(gscore == m, gid, N_GROUPS), axis=0, keepdims=True)
        hit = gid == gsel
        keep = keep | hit
        gscore = jnp.where(hit, neg, gscore)

    cand = jnp.where(keep, sel, neg)
    chosen, weight = [], []
    for _ in range(TOP_K):
        m = _over_experts(jnp.max, cand)
        idx = _over_experts(jnp.min, jnp.where(cand == m, expert, N_EXPERTS))
        hit = expert == idx
        chosen.append(idx)
        weight.append(_over_experts(jnp.sum, jnp.where(hit, scores, 0.0))[0])
        cand = jnp.where(hit, neg, cand)
    norm = functools.reduce(jnp.add, weight)
    wts_ref[0] = jnp.concatenate(weight, axis=0) / norm * ROUTED_SCALE

    onehot = functools.reduce(jnp.add, [jnp.where(expert == idx, 1.0, 0.0) for idx in chosen])
    onehot = _from_groups(onehot)
    t_r = lax.broadcasted_iota(I32, (tm, tm), 0)
    t_c = lax.broadcasted_iota(I32, (tm, tm), 1)
    earlier = jnp.where(t_r < t_c, 1.0, 0.0).astype(BF16)
    rank = jnp.dot(onehot.astype(BF16), earlier, preferred_element_type=F32)
    count = jnp.sum(onehot, axis=1, keepdims=True)
    e_r = lax.broadcasted_iota(I32, (N_EXPERTS, N_EXPERTS), 0)
    e_c = lax.broadcasted_iota(I32, (N_EXPERTS, N_EXPERTS), 1)
    before = jnp.where(e_c < e_r, 1.0, 0.0).astype(BF16)
    c_hi, c_lo = _split_bf16(jnp.broadcast_to(count, (N_EXPERTS, tm)))
    start = (jnp.dot(before, c_hi, preferred_element_type=F32)
             + jnp.dot(before, c_lo, preferred_element_type=F32))
    where_to = _to_groups(rank + start)
    pos = [_over_experts(jnp.sum, jnp.where(expert == idx, where_to, 0.0))[0] for idx in chosen]
    pos_ref[0] = jnp.concatenate(pos, axis=0).astype(I32)
    cnt_ref[0] = count.astype(I32)
    start_ref[0] = start[:, 0:1].astype(I32)


def _router_call(x, w_router, router_bias):
    t, d = x.shape
    nw = t // WINDOW
    per_tok = pl.BlockSpec((1, TOP_K, WINDOW), lambda i: (i, 0, 0))
    per_win = pl.BlockSpec((1, N_EXPERTS, 1), lambda i: (i, 0, 0))
    return pl.pallas_call(
        _router_kernel,
        grid=(nw,),
        in_specs=[pl.BlockSpec((WINDOW, d), lambda i: (i, 0)),
                  pl.BlockSpec((N_EXPERTS, d), lambda i: (0, 0)),
                  pl.BlockSpec((N_EXPERTS, 1), lambda i: (0, 0))],
        out_specs=[per_tok, per_tok, per_win, per_win],
        out_shape=[jax.ShapeDtypeStruct((nw, TOP_K, WINDOW), F32),
                   jax.ShapeDtypeStruct((nw, TOP_K, WINDOW), I32),
                   jax.ShapeDtypeStruct((nw, N_EXPERTS, 1), I32),
                   jax.ShapeDtypeStruct((nw, N_EXPERTS, 1), I32)],
        compiler_params=_params(("parallel",)),
        name="router",
    )(x, w_router.T, router_bias.reshape(N_EXPERTS, 1))


def _segment_copies(cnt_ref, start_ref, off_ref, win, local, remote, sem, to_remote, act):
    def body(e, carry):
        n = cnt_ref[win * N_EXPERTS + e]

        @pl.when(n > 0)
        def _():
            loc = local.at[pl.ds(start_ref[win * N_EXPERTS + e], n)]
            rem = remote.at[pl.ds(off_ref[win * N_EXPERTS + e], n)]
            act(pltpu.make_async_copy(loc, rem, sem) if to_remote
                else pltpu.make_async_copy(rem, loc, sem))
        return carry
    lax.fori_loop(0, N_EXPERTS, body, 0, unroll=4)


def _wait_window(local, remote, sem, to_remote):
    span = remote.at[pl.ds(0, PAIRS)]
    (pltpu.make_async_copy(local, span, sem) if to_remote else pltpu.make_async_copy(span, local, sem)).wait()


def _pad_copies(pad_start_ref, pad_len_ref, zeros, remote, sem, act):
    def body(e, carry):
        n = pad_len_ref[e]

        @pl.when(n > 0)
        def _():
            act(pltpu.make_async_copy(zeros.at[pl.ds(0, n)], remote.at[pl.ds(pad_start_ref[e], n)], sem))
        return carry
    lax.fori_loop(0, N_EXPERTS, body, 0)


def _dispatch_kernel(cnt_ref, start_ref, off_ref, pad_start_ref, pad_len_ref,
                     pos_ref, x_ref, rows_hbm, buf, zeros, sem, pad_sem):
    w = pl.program_id(0)
    last = pl.num_programs(0) - 1
    slot = w % 2
    start = lambda c: c.start()
    wait = lambda c: c.wait()
    copies = functools.partial(_segment_copies, cnt_ref, start_ref, off_ref,
                               remote=rows_hbm, to_remote=True)

    @pl.when(w == 0)
    def _():
        zeros[...] = jnp.zeros_like(zeros)
        _pad_copies(pad_start_ref, pad_len_ref, zeros, rows_hbm, pad_sem, start)

    @pl.when(w >= 2)
    def _():
        _wait_window(buf.at[slot], rows_hbm, sem.at[slot], True)

    def place(t, carry):
        v = x_ref[t]
        for k in range(TOP_K):
            buf[slot, pos_ref[0, 0, k * WINDOW + t]] = v
        return carry
    lax.fori_loop(0, WINDOW, place, 0, unroll=8)
    copies(win=w, local=buf.at[slot], sem=sem.at[slot], act=start)

    @pl.when(w == last)
    def _():
        @pl.when(w >= 1)
        def _():
            _wait_window(buf.at[1 - slot], rows_hbm, sem.at[1 - slot], True)
        _wait_window(buf.at[slot], rows_hbm, sem.at[slot], True)
        _pad_copies(pad_start_ref, pad_len_ref, zeros, rows_hbm, pad_sem, wait)


def _dispatch_call(x_tiles, pos, cnt, start, off, pad_start, pad_len, n_rows):
    t = x_tiles.shape[0]
    nw = t // WINDOW
    grid_spec = pltpu.PrefetchScalarGridSpec(
        num_scalar_prefetch=5,
        grid=(nw,),
        in_specs=[pl.BlockSpec((1, 1, PAIRS), lambda w, *_: (w, 0, 0), memory_space=pltpu.SMEM),
                  pl.BlockSpec((WINDOW, ROW_TILES, LANES), lambda w, *_: (w, 0, 0))],
        out_specs=pl.BlockSpec(memory_space=pl.ANY),
        scratch_shapes=[pltpu.VMEM((2, PAIRS, ROW_TILES, LANES), BF16),
                        pltpu.VMEM((EXPERT_CHUNK, ROW_TILES, LANES), BF16),
                        pltpu.SemaphoreType.DMA((2,)),
                        pltpu.SemaphoreType.DMA(())],
    )
    return pl.pallas_call(
        _dispatch_kernel,
        grid_spec=grid_spec,
        out_shape=jax.ShapeDtypeStruct((n_rows, ROW_TILES, LANES), BF16),
        compiler_params=_params(("arbitrary",)),
        name="dispatch",
    )(cnt, start, off, pad_start, pad_len, pos, x_tiles)


class _RowStream:
    def __init__(self, tr, rows_hbm, out_hbm, weights, row_ref, total, xin, yout, in_sem, out_sem):
        self.tr, self.rows_hbm, self.out_hbm, self.weights = tr, rows_hbm, out_hbm, weights
        self.row_ref, self.total = row_ref, total
        self.xin, self.yout, self.in_sem, self.out_sem = xin, yout, in_sem, out_sem

    def _rows(self, i):
        first = pl.multiple_of(self.row_ref[i] * ROW_TILES, EXPERT_CHUNK * ROW_TILES)
        return pl.ds(first, self.tr * ROW_TILES)

    def _in_copy(self, i, slot):
        return pltpu.make_async_copy(self.rows_hbm.at[self._rows(i)], self.xin.at[slot], self.in_sem.at[slot])

    def _out_copy(self, i, slot):
        return pltpu.make_async_copy(self.yout.at[slot], self.out_hbm.at[self._rows(i)], self.out_sem.at[slot])

    def prime(self):
        @pl.when(self.total > 0)
        def _():
            self._in_copy(0, 0).start()

    def _block(self, i):
        tr, xin, yout = self.tr, self.xin, self.yout
        w1_ref, w3_ref, w2a_ref, w2b_ref = self.weights
        slot = i % 2
        self._in_copy(i, slot).wait()

        @pl.when(i + 1 < self.total)
        def _():
            self._in_copy(i + 1, 1 - slot).start()

        @pl.when(i >= 2)
        def _():
            self._out_copy(i - 2, slot).wait()

        x3 = pltpu.einshape("rcl->crl", xin[slot].reshape(tr, ROW_TILES, LANES))
        x = jnp.concatenate([x3[c] for c in range(ROW_TILES)], axis=1)
        a = jnp.dot(x, w1_ref[0], preferred_element_type=F32)
        b = jnp.dot(x, w3_ref[0], preferred_element_type=F32)
        hid = (jax.nn.silu(a) * b).astype(BF16)
        half = w2a_ref.shape[1]
        y = (jnp.dot(hid[:, :half], w2a_ref[0], preferred_element_type=F32)
             + jnp.dot(hid[:, half:], w2b_ref[0], preferred_element_type=F32))
        y3 = jnp.stack([y[:, c * LANES:(c + 1) * LANES].astype(BF16) for c in range(ROW_TILES)])
        yout[slot] = pltpu.einshape("crl->rcl", y3).reshape(tr * ROW_TILES, LANES)
        self._out_copy(i, slot).start()

    def run(self, first, count):
        def body(j, carry):
            self._block(first + j)
            return carry
        lax.fori_loop(0, count, body, 0)

    def drain(self):
        total = self.total

        @pl.when(total >= 2)
        def _():
            self._out_copy(total - 2, total % 2).wait()

        @pl.when(total >= 1)
        def _():
            self._out_copy(total - 1, (total - 1) % 2).wait()


def _expert_kernel(tile0_ref, n_tile_ref, chunk0_ref, n_chunk_ref, tile_row_ref, chunk_row_ref,
                   rows_hbm, w1_ref, w3_ref, w2a_ref, w2b_ref, out_hbm,
                   xin_t, yout_t, in_sem_t, out_sem_t, xin_c, yout_c, in_sem_c, out_sem_c):
    e = pl.program_id(0)
    last = pl.num_programs(0) - 1
    weights = (w1_ref, w3_ref, w2a_ref, w2b_ref)
    tiles = _RowStream(EXPERT_TILE, rows_hbm, out_hbm, weights, tile_row_ref,
                       tile0_ref[last] + n_tile_ref[last], xin_t, yout_t, in_sem_t, out_sem_t)
    chunks = _RowStream(EXPERT_CHUNK, rows_hbm, out_hbm, weights, chunk_row_ref,
                        chunk0_ref[last] + n_chunk_ref[last], xin_c, yout_c, in_sem_c, out_sem_c)

    @pl.when(e == 0)
    def _():
        tiles.prime()
        chunks.prime()

    tiles.run(tile0_ref[e], n_tile_ref[e])
    chunks.run(chunk0_ref[e], n_chunk_ref[e])

    @pl.when(e == last)
    def _():
        tiles.drain()
        chunks.drain()


def _expert_call(rows2d, tile0, n_tile, chunk0, n_chunk, tile_row, chunk_row, w1, w3, w2a, w2b):
    ne, d, hdim = w1.shape
    per_expert = lambda w: pl.BlockSpec((1,) + w.shape[1:], lambda e, *_: (e, 0, 0))
    stream_scratch = lambda tr: [pltpu.VMEM((2, tr * ROW_TILES, LANES), BF16),
                                 pltpu.VMEM((2, tr * ROW_TILES, LANES), BF16),
                                 pltpu.SemaphoreType.DMA((2,)), pltpu.SemaphoreType.DMA((2,))]
    grid_spec = pltpu.PrefetchScalarGridSpec(
        num_scalar_prefetch=6,
        grid=(ne,),
        in_specs=[pl.BlockSpec(memory_space=pl.ANY),
                  per_expert(w1), per_expert(w3), per_expert(w2a), per_expert(w2b)],
        out_specs=pl.BlockSpec(memory_space=pl.ANY),
        scratch_shapes=stream_scratch(EXPERT_TILE) + stream_scratch(EXPERT_CHUNK),
    )
    return pl.pallas_call(
        _expert_kernel,
        grid_spec=grid_spec,
        out_shape=jax.ShapeDtypeStruct(rows2d.shape, BF16),
        compiler_params=_params(("arbitrary",)),
        name="routed_experts",
    )(tile0, n_tile, chunk0, n_chunk, tile_row, chunk_row, rows2d, w1, w3, w2a, w2b)


def _combine_kernel(cnt_ref, start_ref, off_ref, pos_ref, wts_ref, rows_hbm, s_ref, g_ref, b_ref,
                    o32_ref, o16_ref, buf, acc, sem):
    w = pl.program_id(0)
    last = pl.num_programs(0) - 1
    slot = w % 2
    start = lambda c: c.start()
    wait = lambda c: c.wait()
    copies = functools.partial(_segment_copies, cnt_ref, start_ref, off_ref,
                               remote=rows_hbm, to_remote=False)

    @pl.when(w == 0)
    def _():
        copies(win=w, local=buf.at[slot], sem=sem.at[slot], act=start)

    @pl.when(w < last)
    def _():
        copies(win=w + 1, local=buf.at[1 - slot], sem=sem.at[1 - slot], act=start)

    _wait_window(buf.at[slot], rows_hbm, sem.at[slot], False)

    def token(t, carry):
        total = wts_ref[0, 0, t] * buf[slot, pos_ref[0, 0, t]].astype(F32)
        for k in range(1, TOP_K):
            total = total + wts_ref[0, 0, k * WINDOW + t] * buf[slot, pos_ref[0, 0, k * WINDOW + t]].astype(F32)
        acc[t] = total
        return carry
    lax.fori_loop(0, WINDOW, token, 0, unroll=8)

    by_chunk = pltpu.einshape("rcl->crl", acc[...])
    routed = jnp.concatenate([by_chunk[c] for c in range(ROW_TILES)], axis=1)
    out = _layer_norm_rows(s_ref[...] + routed, g_ref[...], b_ref[...])
    o32_ref[...] = out
    o16_ref[...] = out.astype(BF16)


def _combine_call(rows_tiles, pos, wts, cnt, start, off, s, g, b):
    t, d = s.shape
    nw = t // WINDOW
    smem_block = pl.BlockSpec((1, 1, PAIRS), lambda w, *_: (w, 0, 0), memory_space=pltpu.SMEM)
    row = pl.BlockSpec((WINDOW, d), lambda w, *_: (w, 0))
    vec = pl.BlockSpec((1, d), lambda w, *_: (0, 0))
    grid_spec = pltpu.PrefetchScalarGridSpec(
        num_scalar_prefetch=3,
        grid=(nw,),
        in_specs=[smem_block, smem_block, pl.BlockSpec(memory_space=pl.ANY), row, vec, vec],
        out_specs=[row, row],
        scratch_shapes=[pltpu.VMEM((2, PAIRS, ROW_TILES, LANES), BF16),
                        pltpu.VMEM((WINDOW, ROW_TILES, LANES), F32),
                        pltpu.SemaphoreType.DMA((2,))],
    )
    return pl.pallas_call(
        _combine_kernel,
        grid_spec=grid_spec,
        out_shape=[jax.ShapeDtypeStruct((t, d), F32), jax.ShapeDtypeStruct((t, d), BF16)],
        compiler_params=_params(("arbitrary",)),
        name="combine_ln",
    )(cnt, start, off, pos, wts, rows_tiles, s, g.reshape(1, d), b.reshape(1, d))


def _block_rows(first_block, n_blocks, first_row, rows_per_block, capacity):
    block = jnp.arange(capacity, dtype=I32)
    ends = first_block + n_blocks
    owner = jnp.minimum(jnp.sum((ends[None, :] <= block[:, None]).astype(I32), axis=1), N_EXPERTS - 1)
    return (first_row[owner] + (block - first_block[owner]) * rows_per_block).astype(I32)


def _row_layout(cnt, n_tokens):
    cnt2 = cnt.reshape(-1, N_EXPERTS)
    total = jnp.sum(cnt2, axis=0)
    padded = (total + EXPERT_CHUNK - 1) // EXPERT_CHUNK * EXPERT_CHUNK
    base = jnp.cumsum(padded) - padded
    off = base[None, :] + jnp.cumsum(cnt2, axis=0) - cnt2
    n_tile = padded // EXPERT_TILE
    n_chunk = (padded - n_tile * EXPERT_TILE) // EXPERT_CHUNK
    tile0 = jnp.cumsum(n_tile) - n_tile
    chunk0 = jnp.cumsum(n_chunk) - n_chunk
    tile_row = _block_rows(tile0, n_tile, base, EXPERT_TILE, n_tokens * TOP_K // EXPERT_TILE + N_EXPERTS)
    chunk_row = _block_rows(chunk0, n_chunk, base + n_tile * EXPERT_TILE, EXPERT_CHUNK,
                            N_EXPERTS * (EXPERT_TILE // EXPERT_CHUNK - 1))
    as_i32 = lambda *arrays: tuple(a.astype(I32) for a in arrays)
    return (as_i32(off.reshape(-1), base + total, padded - total),
            as_i32(tile0, n_tile, chunk0, n_chunk, tile_row, chunk_row))


def _moe_block(x32, x16, s, w_router, router_bias, w1_16, w3_16, w2a_16, w2b_16, g, b):
    t, d = x32.shape
    nw = t // WINDOW
    n_rows = t * TOP_K + N_EXPERTS * EXPERT_CHUNK
    wts, pos, cnt, start = _router_call(x32, w_router, router_bias)
    (off, pad_start, pad_len), blocks = _row_layout(cnt, t)
    cnt = cnt.reshape(-1)
    start = start.reshape(-1)
    pos = pos.reshape(nw, 1, PAIRS)
    wts = wts.reshape(nw, 1, PAIRS)
    rows = _dispatch_call(x16.reshape(t, ROW_TILES, LANES), pos, cnt, start, off, pad_start, pad_len, n_rows)
    out_rows = _expert_call(rows.reshape(n_rows * ROW_TILES, LANES), *blocks, w1_16, w3_16, w2a_16, w2b_16)
    return _combine_call(out_rows.reshape(n_rows, ROW_TILES, LANES), pos, wts, cnt, start, off, s, g, b)


def kernel(x, in_ln_g, in_ln_b, w_in, w_spatial, b_spatial, w_conv, w_pool, pool_scale,
           w_gate, b_gate, w_proj_a, w_proj_b, w_proj_c, w_out, ln1_g, ln1_b,
           w_router, router_bias, w_expert_gate, w_expert_up, w_expert_down,
           w_shared_gate, w_shared_up, w_shared_down, ln2_g, ln2_b):
    bsz, seq, d = x.shape
    x32, x16 = _ln_call(x.reshape(bsz * seq, d), in_ln_g, in_ln_b)
    for l in range(DEPTH):
        ya, yb, yc, w1_16, w2a_16 = _mix_tokens_call(
            x16, w_in[l].astype(BF16), w_spatial[l], b_spatial[l], w_conv[l], w_pool[l].astype(BF16),
            pool_scale[l], l, w_expert_gate, w_expert_down)
        merged, w3_16, w2b_16 = _merge_call(
            x16, ya, yb, yc, w_gate[l].astype(BF16), b_gate[l], w_proj_a[l].astype(BF16),
            w_proj_b[l].astype(BF16), w_proj_c[l].astype(BF16), l, w_expert_up, w_expert_down)
        x32, x16, s = _out_proj_call(merged, x32, w_out[l].astype(BF16), ln1_g[l], ln1_b[l],
                                     w_shared_gate[l].astype(BF16), w_shared_up[l].astype(BF16),
                                     w_shared_down[l].astype(BF16))
        x32, x16 = _moe_block(x32, x16, s, w_router[l], router_bias[l], w1_16, w3_16, w2a_16, w2b_16,
                              ln2_g[l], ln2_b[l])
    return x32.reshape(bsz, seq, d)
```

```python
import functools

import jax
import jax.numpy as jnp
from jax import lax
from jax.experimental import pallas as pl
from jax.experimental.pallas import tpu as pltpu

F32 = jnp.float32
BF16 = jnp.bfloat16
I32 = jnp.int32

D_MODEL = 2048
DEPTH = 2
CHUNK = 64
A_WIDTH = 1024
A_BLOCK = 128
N_HEAD_GROUPS = 4
HEAD = A_WIDTH // N_HEAD_GROUPS
N_STREAMS = 6
CONV_W = 3
N_BRANCH = 3
N_EXPERTS = 64
TOP_K = 8
N_GROUPS = 8
GROUP_SIZE = N_EXPERTS // N_GROUPS
TOPK_GROUPS = 4
ROUTED_SCALE = 2.5
ALPHA = (2.0 * DEPTH) ** 0.25
LN_EPS = 1e-5

LANES = 128
ROW_TILES = D_MODEL // LANES
MERGE_COLS = 512
WINDOW = 256
PAIRS = WINDOW * TOP_K
EXPERT_TILE = 512
EXPERT_CHUNK = 128
VMEM_LIMIT = 56 * 1024 * 1024

def _params(sem, vmem=VMEM_LIMIT):
    return pltpu.CompilerParams(dimension_semantics=sem, vmem_limit_bytes=vmem)


def _gelu(x):
    return 0.5 * x * (1.0 + lax.erf(x * (0.5 ** 0.5)))


def _layer_norm_rows(r, g, b):
    mu = jnp.mean(r, axis=-1, keepdims=True)
    c = r - mu
    var = jnp.mean(c * c, axis=-1, keepdims=True)
    return c * lax.rsqrt(var + LN_EPS) * g + b


def _ln_kernel(x_ref, g_ref, b_ref, o32_ref, o16_ref):
    y = _layer_norm_rows(x_ref[...], g_ref[...], b_ref[...])
    o32_ref[...] = y
    o16_ref[...] = y.astype(BF16)


def _ln_call(x, g, b, tm=256):
    t, d = x.shape
    row = pl.BlockSpec((tm, d), lambda i: (i, 0))
    vec = pl.BlockSpec((1, d), lambda i: (0, 0))
    return pl.pallas_call(
        _ln_kernel,
        grid=(t // tm,),
        in_specs=[row, vec, vec],
        out_specs=[row, row],
        out_shape=[jax.ShapeDtypeStruct((t, d), F32), jax.ShapeDtypeStruct((t, d), BF16)],
        compiler_params=_params(("parallel",)),
        name="layer_norm",
    )(x, g.reshape(1, d), b.reshape(1, d))


def _mix_tokens_kernel(x_ref, wu_ref, wv_ref, wbg_ref, wcg_ref, wh_ref, wpp_ref,
                       ws_ref, bs_ref, wc_ref, wp_ref, ps_ref, ew_ref, ed_ref,
                       ya_ref, yb_ref, yc_ref, ew16_ref, ed16_ref, zc_ref, pc_ref):
    g = pl.program_id(0)
    i = pl.program_id(1)
    tm = x_ref.shape[0]
    ew16_ref[...] = ew_ref[0].astype(BF16)
    ed16_ref[...] = ed_ref[0].astype(BF16)

    @pl.when(i == 0)
    def _():
        zc_ref[...] = jnp.zeros_like(zc_ref)
        pc_ref[...] = jnp.zeros_like(pc_ref)

    x = x_ref[...]
    proj = lambda w_ref: jnp.dot(x, w_ref[...], preferred_element_type=F32)

    u = _gelu(proj(wu_ref))
    v = _gelu(proj(wv_ref))
    mu = jnp.mean(v, axis=-1, keepdims=True)
    vc = v - mu
    var = jnp.mean(vc * vc, axis=-1, keepdims=True)
    vn = (vc * lax.rsqrt(var + LN_EPS)).astype(BF16)
    qi = lax.broadcasted_iota(I32, (A_BLOCK, A_BLOCK), 0) // CHUNK
    kj = lax.broadcasted_iota(I32, (A_BLOCK, A_BLOCK), 1) // CHUNK
    ws = jnp.where(kj <= qi, ws_ref[0], 0.0).astype(BF16)
    bs = bs_ref[0]
    for n in range(tm // A_BLOCK):
        rows = slice(n * A_BLOCK, (n + 1) * A_BLOCK)
        mixed = jnp.dot(ws, vn[rows], preferred_element_type=F32) + bs
        ya_ref[rows, :] = (u[rows] * mixed).astype(BF16)

    z = proj(wcg_ref) * proj(wh_ref)
    row = lax.broadcasted_iota(I32, (tm, 1), 0)
    prev1 = zc_ref[7:8, :]
    prev2 = zc_ref[6:7, :]
    z1 = jnp.where(row == 0, prev1, pltpu.roll(z, 1, 0))
    z2 = jnp.where(row == 0, prev2, jnp.where(row == 1, prev1, pltpu.roll(z, 2, 0)))
    conv = wc_ref[0:1, :] * z2 + wc_ref[1:2, :] * z1 + wc_ref[2:3, :] * z
    yb_ref[...] = (proj(wbg_ref) * conv).astype(BF16)
    zc_ref[...] = z[tm - 8:, :]

    p = proj(wpp_ref)
    win = lax.shift_left(jnp.int32(2), g)
    p16 = p.astype(BF16)
    pext = jnp.concatenate([pc_ref[...], p16], axis=0)
    rr = lax.broadcasted_iota(I32, (tm, tm + A_BLOCK), 0) + A_BLOCK
    cc = lax.broadcasted_iota(I32, (tm, tm + A_BLOCK), 1)
    band = jnp.where((cc <= rr) & (cc > rr - win), 1.0, 0.0).astype(BF16)
    total = jnp.dot(band, pext, preferred_element_type=F32)
    count = jnp.minimum(i * tm + row + 1, win).astype(F32)
    pooled = (total / count - p).astype(BF16)
    yc = jnp.dot(pooled, wp_ref[0], preferred_element_type=F32) * ps_ref[...]
    yc_ref[...] = yc.astype(BF16)
    pc_ref[...] = p16[tm - A_BLOCK:, :]


def _cast_stream_specs(w, layer, n_steps, step_of, halves=1, half=0):
    _, ne, r, c = w.shape
    per = ne // n_steps
    assert per * n_steps == ne and r % halves == 0
    rows = r // halves
    return (pl.BlockSpec((1, per, rows, c), lambda *idx: (layer, step_of(*idx), half, 0)),
            pl.BlockSpec((per, rows, c), lambda *idx: (step_of(*idx), 0, 0)),
            jax.ShapeDtypeStruct((ne, rows, c), BF16))


def _mix_tokens_call(xb, w_in16, w_s, b_s, w_conv, w_pool16, pool_scale, layer, expert_w, expert_down, tm=512):
    t, d = xb.shape
    nt = t // tm
    step_of = lambda g, i, *_: g * nt + i
    ew_in, ew_out, ew_shape = _cast_stream_specs(expert_w, layer, N_HEAD_GROUPS * nt, step_of)
    ed_in, ed_out, ed_shape = _cast_stream_specs(expert_down, layer, N_HEAD_GROUPS * nt, step_of, 2, 0)
    stream = lambda s: pl.BlockSpec((d, HEAD), lambda g, i: (0, s * N_HEAD_GROUPS + g))
    grp = lambda *shape: pl.BlockSpec((1,) + shape, lambda g, i: (g,) + (0,) * len(shape))
    out_spec = pl.BlockSpec((tm, HEAD), lambda g, i: (i, g))
    out_shape = jax.ShapeDtypeStruct((t, A_WIDTH), BF16)
    return pl.pallas_call(
        _mix_tokens_kernel,
        grid=(N_HEAD_GROUPS, t // tm),
        in_specs=[pl.BlockSpec((tm, d), lambda g, i: (i, 0))]
        + [stream(s) for s in range(N_STREAMS)]
        + [grp(A_BLOCK, A_BLOCK), grp(A_BLOCK, 1),
           pl.BlockSpec((CONV_W, HEAD), lambda g, i: (0, g)),
           grp(HEAD, HEAD),
           pl.BlockSpec((1, HEAD), lambda g, i: (0, g)),
           ew_in, ed_in],
        out_specs=[out_spec, out_spec, out_spec, ew_out, ed_out],
        out_shape=[out_shape, out_shape, out_shape, ew_shape, ed_shape],
        scratch_shapes=[pltpu.VMEM((8, HEAD), F32), pltpu.VMEM((A_BLOCK, HEAD), BF16)],
        compiler_params=_params(("arbitrary", "arbitrary")),
        name="mix_tokens",
    )(xb, *([w_in16] * N_STREAMS), w_s, b_s.reshape(N_HEAD_GROUPS, A_BLOCK, 1), w_conv,
      w_pool16, pool_scale.reshape(1, A_WIDTH), expert_w, expert_down)


def _merge_kernel(x_ref, ya_ref, yb_ref, yc_ref, wg0_ref, wg1_ref, wg2_ref, bg0_ref, bg1_ref, bg2_ref,
                  wa_ref, wb_ref, wc_ref, ew_ref, ed_ref, m_ref, ew16_ref, ed16_ref):
    ew16_ref[...] = ew_ref[0].astype(BF16)
    ed16_ref[...] = ed_ref[0].astype(BF16)
    x = x_ref[...]
    gate = lambda w_ref, b_ref: jax.nn.sigmoid(
        jnp.dot(x, w_ref[...], preferred_element_type=F32) + b_ref[...])
    m = gate(wg0_ref, bg0_ref) * jnp.dot(ya_ref[...], wa_ref[...], preferred_element_type=F32)
    m = m + gate(wg1_ref, bg1_ref) * jnp.dot(yb_ref[...], wb_ref[...], preferred_element_type=F32)
    m = m + gate(wg2_ref, bg2_ref) * jnp.dot(yc_ref[...], wc_ref[...], preferred_element_type=F32)
    m_ref[...] = m.astype(BF16)


def _merge_call(xb, ya, yb, yc, w_gate16, b_gate, wa16, wb16, wc16, layer, expert_w, expert_down, tm=512):
    t, d = xb.shape
    nj = d // MERGE_COLS
    nt = t // tm
    step_of = lambda j, i, *_: j * nt + i
    ew_in, ew_out, ew_shape = _cast_stream_specs(expert_w, layer, nj * nt, step_of)
    ed_in, ed_out, ed_shape = _cast_stream_specs(expert_down, layer, nj * nt, step_of, 2, 1)
    tok = lambda width: pl.BlockSpec((tm, width), lambda j, i: (i, 0))
    gate_w = lambda br: pl.BlockSpec((d, MERGE_COLS), lambda j, i: (0, br * nj + j))
    gate_b = lambda br: pl.BlockSpec((1, MERGE_COLS), lambda j, i: (0, br * nj + j))
    branch_w = pl.BlockSpec((A_WIDTH, MERGE_COLS), lambda j, i: (0, j))
    b2 = b_gate.reshape(1, N_BRANCH * d)
    return pl.pallas_call(
        _merge_kernel,
        grid=(nj, t // tm),
        in_specs=[tok(d), tok(A_WIDTH), tok(A_WIDTH), tok(A_WIDTH)]
        + [gate_w(br) for br in range(N_BRANCH)] + [gate_b(br) for br in range(N_BRANCH)]
        + [branch_w] * N_BRANCH + [ew_in, ed_in],
        out_specs=[pl.BlockSpec((tm, MERGE_COLS), lambda j, i: (i, j)), ew_out, ed_out],
        out_shape=[jax.ShapeDtypeStruct((t, d), BF16), ew_shape, ed_shape],
        compiler_params=_params(("arbitrary", "arbitrary")),
        name="merge_branches",
    )(xb, ya, yb, yc, w_gate16, w_gate16, w_gate16, b2, b2, b2, wa16, wb16, wc16, expert_w, expert_down)


def _out_proj_kernel(m_ref, x_ref, w_ref, g_ref, b_ref, wg_ref, wu_ref, wd_ref, o32_ref, o16_ref, s_ref):
    r = ALPHA * x_ref[...] + jnp.dot(m_ref[...], w_ref[...], preferred_element_type=F32)
    y = _layer_norm_rows(r, g_ref[...], b_ref[...])
    yb = y.astype(BF16)
    o32_ref[...] = y
    o16_ref[...] = yb
    a = jnp.dot(yb, wg_ref[...], preferred_element_type=F32)
    b = jnp.dot(yb, wu_ref[...], preferred_element_type=F32)
    hid = (jax.nn.silu(a) * b).astype(BF16)
    s_ref[...] = ALPHA * y + jnp.dot(hid, wd_ref[...], preferred_element_type=F32)


def _out_proj_call(merged, x, w_out16, g, b, wg16, wu16, wd16, tm=512):
    t, d = x.shape
    hdim = wg16.shape[1]
    row = pl.BlockSpec((tm, d), lambda i: (i, 0))
    vec = pl.BlockSpec((1, d), lambda i: (0, 0))
    whole = lambda r, c: pl.BlockSpec((r, c), lambda i: (0, 0), pipeline_mode=pl.Buffered(1))
    return pl.pallas_call(
        _out_proj_kernel,
        grid=(t // tm,),
        in_specs=[row, row, whole(d, d), vec, vec, whole(d, hdim), whole(d, hdim), whole(hdim, d)],
        out_specs=[row, row, row],
        out_shape=[jax.ShapeDtypeStruct((t, d), F32), jax.ShapeDtypeStruct((t, d), BF16),
                   jax.ShapeDtypeStruct((t, d), F32)],
        compiler_params=_params(("parallel",)),
        name="out_proj_ln_shared",
    )(merged, x, w_out16, g.reshape(1, d), b.reshape(1, d), wg16, wu16, wd16)


def _dot_nt(a, b):
    return lax.dot_general(a, b, (((1,), (1,)), ((), ())), preferred_element_type=F32)


def _split_bf16(v):
    hi = v.astype(BF16)
    return hi, (v - hi.astype(F32)).astype(BF16)


def _to_groups(v):
    return jnp.stack([v[g * GROUP_SIZE:(g + 1) * GROUP_SIZE] for g in range(N_GROUPS)])


def _from_groups(v):
    return jnp.concatenate([v[g] for g in range(N_GROUPS)], axis=0)


def _over_experts(fn, v):
    return fn(fn(v, axis=0, keepdims=True), axis=1, keepdims=True)


def _router_kernel(x_ref, wt_ref, b_ref, wts_ref, pos_ref, cnt_ref, start_ref):
    tm = x_ref.shape[0]
    neg = -jnp.inf
    xh, xl = _split_bf16(x_ref[...])
    wh, wl = _split_bf16(wt_ref[...])
    logits = _dot_nt(wh, xh) + (_dot_nt(wh, xl) + _dot_nt(wl, xh))
    scores2d = jax.nn.sigmoid(logits)
    scores = _to_groups(scores2d)
    sel = _to_groups(scores2d + b_ref[...])
    shape3 = (N_GROUPS, GROUP_SIZE, tm)
    member = lax.broadcasted_iota(I32, shape3, 1)
    expert = lax.broadcasted_iota(I32, shape3, 0) * GROUP_SIZE + member

    m1 = jnp.max(sel, axis=1, keepdims=True)
    i1 = jnp.min(jnp.where(sel == m1, member, GROUP_SIZE), axis=1, keepdims=True)
    m2 = jnp.max(jnp.where(member == i1, neg, sel), axis=1, keepdims=True)
    gscore = m1 + m2

    gid = lax.broadcasted_iota(I32, (N_GROUPS, 1, tm), 0)
    keep = jnp.zeros((N_GROUPS, 1, tm), jnp.bool_)
    for _ in range(TOPK_GROUPS):
        m = jnp.max(gscore, axis=0, keepdims=True)
        gsel = jnp.min(jnp.where(gscore == m, gid, N_GROUPS), axis=0, keepdims=True)
        hit = gid == gsel
        keep = keep | hit
        gscore = jnp.where(hit, neg, gscore)

    cand = jnp.where(keep, sel, neg)
    chosen, weight = [], []
    for _ in range(TOP_K):
        m = _over_experts(jnp.max, cand)
        idx = _over_experts(jnp.min, jnp.where(cand == m, expert, N_EXPERTS))
        hit = expert == idx
        chosen.append(idx)
        weight.append(_over_experts(jnp.sum, jnp.where(hit, scores, 0.0))[0])
        cand = jnp.where(hit, neg, cand)
    norm = functools.reduce(jnp.add, weight)
    wts_ref[0] = jnp.concatenate(weight, axis=0) / norm * ROUTED_SCALE

    onehot = functools.reduce(jnp.add, [jnp.where(expert == idx, 1.0, 0.0) for idx in chosen])
    onehot = _from_groups(onehot)
    t_r = lax.broadcasted_iota(I32, (tm, tm), 0)
    t_c = lax.broadcasted_iota(I32, (tm, tm), 1)
    earlier = jnp.where(t_r < t_c, 1.0, 0.0).astype(BF16)
    rank = jnp.dot(onehot.astype(BF16), earlier, preferred_element_type=F32)
    count = jnp.sum(onehot, axis=1, keepdims=True)
    e_r = lax.broadcasted_iota(I32, (N_EXPERTS, N_EXPERTS), 0)
    e_c = lax.broadcasted_iota(I32, (N_EXPERTS, N_EXPERTS), 1)
    before = jnp.where(e_c < e_r, 1.0, 0.0).astype(BF16)
    c_hi, c_lo = _split_bf16(jnp.broadcast_to(count, (N_EXPERTS, tm)))
    start = (jnp.dot(before, c_hi, preferred_element_type=F32)
             + jnp.dot(before, c_lo, preferred_element_type=F32))
    where_to = _to_groups(rank + start)
    pos = [_over_experts(jnp.sum, jnp.where(expert == idx, where_to, 0.0))[0] for idx in chosen]
    pos_ref[0] = jnp.concatenate(pos, axis=0).astype(I32)
    cnt_ref[0] = count.astype(I32)
    start_ref[0] = start[:, 0:1].astype(I32)


def _router_call(x, w_router, router_bias):
    t, d = x.shape
    nw = t // WINDOW
    per_tok = pl.BlockSpec((1, TOP_K, WINDOW), lambda i: (i, 0, 0))
    per_win = pl.BlockSpec((1, N_EXPERTS, 1), lambda i: (i, 0, 0))
    return pl.pallas_call(
        _router_kernel,
        grid=(nw,),
        in_specs=[pl.BlockSpec((WINDOW, d), lambda i: (i, 0)),
                  pl.BlockSpec((N_EXPERTS, d), lambda i: (0, 0)),
                  pl.BlockSpec((N_EXPERTS, 1), lambda i: (0, 0))],
        out_specs=[per_tok, per_tok, per_win, per_win],
        out_shape=[jax.ShapeDtypeStruct((nw, TOP_K, WINDOW), F32),
                   jax.ShapeDtypeStruct((nw, TOP_K, WINDOW), I32),
                   jax.ShapeDtypeStruct((nw, N_EXPERTS, 1), I32),
                   jax.ShapeDtypeStruct((nw, N_EXPERTS, 1), I32)],
        compiler_params=_params(("parallel",)),
        name="router",
    )(x, w_router.T, router_bias.reshape(N_EXPERTS, 1))


def _segment_copies(cnt_ref, start_ref, off_ref, win, local, remote, sem, to_remote, act):
    def body(e, carry):
        n = cnt_ref[win * N_EXPERTS + e]

        @pl.when(n > 0)
        def _():
            loc = local.at[pl.ds(start_ref[win * N_EXPERTS + e], n)]
            rem = remote.at[pl.ds(off_ref[win * N_EXPERTS + e], n)]
            act(pltpu.make_async_copy(loc, rem, sem) if to_remote
                else pltpu.make_async_copy(rem, loc, sem))
        return carry
    lax.fori_loop(0, N_EXPERTS, body, 0, unroll=4)


def _wait_window(local, remote, sem, to_remote):
    span = remote.at[pl.ds(0, PAIRS)]
    (pltpu.make_async_copy(local, span, sem) if to_remote else pltpu.make_async_copy(span, local, sem)).wait()


def _pad_copies(pad_start_ref, pad_len_ref, zeros, remote, sem, act):
    def body(e, carry):
        n = pad_len_ref[e]

        @pl.when(n > 0)
        def _():
            act(pltpu.make_async_copy(zeros.at[pl.ds(0, n)], remote.at[pl.ds(pad_start_ref[e], n)], sem))
        return carry
    lax.fori_loop(0, N_EXPERTS, body, 0)


def _dispatch_kernel(cnt_ref, start_ref, off_ref, pad_start_ref, pad_len_ref,
                     pos_ref, x_ref, rows_hbm, buf, zeros, sem, pad_sem):
    w = pl.program_id(0)
    last = pl.num_programs(0) - 1
    slot = w % 2
    start = lambda c: c.start()
    wait = lambda c: c.wait()
    copies = functools.partial(_segment_copies, cnt_ref, start_ref, off_ref,
                               remote=rows_hbm, to_remote=True)

    @pl.when(w == 0)
    def _():
        zeros[...] = jnp.zeros_like(zeros)
        _pad_copies(pad_start_ref, pad_len_ref, zeros, rows_hbm, pad_sem, start)

    @pl.when(w >= 2)
    def _():
        _wait_window(buf.at[slot], rows_hbm, sem.at[slot], True)

    def place(t, carry):
        v = x_ref[t]
        for k in range(TOP_K):
            buf[slot, pos_ref[0, 0, k * WINDOW + t]] = v
        return carry
    lax.fori_loop(0, WINDOW, place, 0, unroll=8)
    copies(win=w, local=buf.at[slot], sem=sem.at[slot], act=start)

    @pl.when(w == last)
    def _():
        @pl.when(w >= 1)
        def _():
            _wait_window(buf.at[1 - slot], rows_hbm, sem.at[1 - slot], True)
        _wait_window(buf.at[slot], rows_hbm, sem.at[slot], True)
        _pad_copies(pad_start_ref, pad_len_ref, zeros, rows_hbm, pad_sem, wait)


def _dispatch_call(x_tiles, pos, cnt, start, off, pad_start, pad_len, n_rows):
    t = x_tiles.shape[0]
    nw = t // WINDOW
    grid_spec = pltpu.PrefetchScalarGridSpec(
        num_scalar_prefetch=5,
        grid=(nw,),
        in_specs=[pl.BlockSpec((1, 1, PAIRS), lambda w, *_: (w, 0, 0), memory_space=pltpu.SMEM),
                  pl.BlockSpec((WINDOW, ROW_TILES, LANES), lambda w, *_: (w, 0, 0))],
        out_specs=pl.BlockSpec(memory_space=pl.ANY),
        scratch_shapes=[pltpu.VMEM((2, PAIRS, ROW_TILES, LANES), BF16),
                        pltpu.VMEM((EXPERT_CHUNK, ROW_TILES, LANES), BF16),
                        pltpu.SemaphoreType.DMA((2,)),
                        pltpu.SemaphoreType.DMA(())],
    )
    return pl.pallas_call(
        _dispatch_kernel,
        grid_spec=grid_spec,
        out_shape=jax.ShapeDtypeStruct((n_rows, ROW_TILES, LANES), BF16),
        compiler_params=_params(("arbitrary",)),
        name="dispatch",
    )(cnt, start, off, pad_start, pad_len, pos, x_tiles)


class _RowStream:
    def __init__(self, tr, rows_hbm, out_hbm, weights, row_ref, total, xin, yout, in_sem, out_sem):
        self.tr, self.rows_hbm, self.out_hbm, self.weights = tr, rows_hbm, out_hbm, weights
        self.row_ref, self.total = row_ref, total
        self.xin, self.yout, self.in_sem, self.out_sem = xin, yout, in_sem, out_sem

    def _rows(self, i):
        first = pl.multiple_of(self.row_ref[i] * ROW_TILES, EXPERT_CHUNK * ROW_TILES)
        return pl.ds(first, self.tr * ROW_TILES)

    def _in_copy(self, i, slot):
        return pltpu.make_async_copy(self.rows_hbm.at[self._rows(i)], self.xin.at[slot], self.in_sem.at[slot])

    def _out_copy(self, i, slot):
        return pltpu.make_async_copy(self.yout.at[slot], self.out_hbm.at[self._rows(i)], self.out_sem.at[slot])

    def prime(self):
        @pl.when(self.total > 0)
        def _():
            self._in_copy(0, 0).start()

    def _block(self, i):
        tr, xin, yout = self.tr, self.xin, self.yout
        w1_ref, w3_ref, w2a_ref, w2b_ref = self.weights
        slot = i % 2
        self._in_copy(i, slot).wait()

        @pl.when(i + 1 < self.total)
        def _():
            self._in_copy(i + 1, 1 - slot).start()

        @pl.when(i >= 2)
        def _():
            self._out_copy(i - 2, slot).wait()

        x3 = jnp.swapaxes(xin[slot].reshape(tr, ROW_TILES, LANES), 0, 1)
        x = jnp.concatenate([x3[c] for c in range(ROW_TILES)], axis=1)
        a = jnp.dot(x, w1_ref[0], preferred_element_type=F32)
        b = jnp.dot(x, w3_ref[0], preferred_element_type=F32)
        hid = (jax.nn.silu(a) * b).astype(BF16)
        half = w2a_ref.shape[1]
        y = (jnp.dot(hid[:, :half], w2a_ref[0], preferred_element_type=F32)
             + jnp.dot(hid[:, half:], w2b_ref[0], preferred_element_type=F32))
        y3 = jnp.stack([y[:, c * LANES:(c + 1) * LANES].astype(BF16) for c in range(ROW_TILES)])
        yout[slot] = jnp.swapaxes(y3, 0, 1).reshape(tr * ROW_TILES, LANES)
        self._out_copy(i, slot).start()

    def run(self, first, count):
        def body(j, carry):
            self._block(first + j)
            return carry
        lax.fori_loop(0, count, body, 0)

    def drain(self):
        total = self.total

        @pl.when(total >= 2)
        def _():
            self._out_copy(total - 2, total % 2).wait()

        @pl.when(total >= 1)
        def _():
            self._out_copy(total - 1, (total - 1) % 2).wait()


def _expert_kernel(tile0_ref, n_tile_ref, chunk0_ref, n_chunk_ref, tile_row_ref, chunk_row_ref,
                   rows_hbm, w1_ref, w3_ref, w2a_ref, w2b_ref, out_hbm,
                   xin_t, yout_t, in_sem_t, out_sem_t, xin_c, yout_c, in_sem_c, out_sem_c):
    e = pl.program_id(0)
    last = pl.num_programs(0) - 1
    weights = (w1_ref, w3_ref, w2a_ref, w2b_ref)
    tiles = _RowStream(EXPERT_TILE, rows_hbm, out_hbm, weights, tile_row_ref,
                       tile0_ref[last] + n_tile_ref[last], xin_t, yout_t, in_sem_t, out_sem_t)
    chunks = _RowStream(EXPERT_CHUNK, rows_hbm, out_hbm, weights, chunk_row_ref,
                        chunk0_ref[last] + n_chunk_ref[last], xin_c, yout_c, in_sem_c, out_sem_c)

    @pl.when(e == 0)
    def _():
        tiles.prime()
        chunks.prime()

    tiles.run(tile0_ref[e], n_tile_ref[e])
    chunks.run(chunk0_ref[e], n_chunk_ref[e])

    @pl.when(e == last)
    def _():
        tiles.drain()
        chunks.drain()


def _expert_call(rows2d, tile0, n_tile, chunk0, n_chunk, tile_row, chunk_row, w1, w3, w2a, w2b):
    ne, d, hdim = w1.shape
    per_expert = lambda w: pl.BlockSpec((1,) + w.shape[1:], lambda e, *_: (e, 0, 0))
    stream_scratch = lambda tr: [pltpu.VMEM((2, tr * ROW_TILES, LANES), BF16),
                                 pltpu.VMEM((2, tr * ROW_TILES, LANES), BF16),
                                 pltpu.SemaphoreType.DMA((2,)), pltpu.SemaphoreType.DMA((2,))]
    grid_spec = pltpu.PrefetchScalarGridSpec(
        num_scalar_prefetch=6,
        grid=(ne,),
        in_specs=[pl.BlockSpec(memory_space=pl.ANY),
                  per_expert(w1), per_expert(w3), per_expert(w2a), per_expert(w2b)],
        out_specs=pl.BlockSpec(memory_space=pl.ANY),
        scratch_shapes=stream_scratch(EXPERT_TILE) + stream_scratch(EXPERT_CHUNK),
    )
    return pl.pallas_call(
        _expert_kernel,
        grid_spec=grid_spec,
        out_shape=jax.ShapeDtypeStruct(rows2d.shape, BF16),
        compiler_params=_params(("arbitrary",)),
        name="routed_experts",
    )(tile0, n_tile, chunk0, n_chunk, tile_row, chunk_row, rows2d, w1, w3, w2a, w2b)


def _combine_kernel(cnt_ref, start_ref, off_ref, pos_ref, wts_ref, rows_hbm, s_ref, g_ref, b_ref,
                    o32_ref, o16_ref, buf, acc, sem):
    w = pl.program_id(0)
    last = pl.num_programs(0) - 1
    slot = w % 2
    start = lambda c: c.start()
    wait = lambda c: c.wait()
    copies = functools.partial(_segment_copies, cnt_ref, start_ref, off_ref,
                               remote=rows_hbm, to_remote=False)

    @pl.when(w == 0)
    def _():
        copies(win=w, local=buf.at[slot], sem=sem.at[slot], act=start)

    @pl.when(w < last)
    def _():
        copies(win=w + 1, local=buf.at[1 - slot], sem=sem.at[1 - slot], act=start)

    _wait_window(buf.at[slot], rows_hbm, sem.at[slot], False)

    def token(t, carry):
        total = wts_ref[0, 0, t] * buf[slot, pos_ref[0, 0, t]].astype(F32)
        for k in range(1, TOP_K):
            total = total + wts_ref[0, 0, k * WINDOW + t] * buf[slot, pos_ref[0, 0, k * WINDOW + t]].astype(F32)
        acc[t] = total
        return carry
    lax.fori_loop(0, WINDOW, token, 0, unroll=8)

    by_chunk = jnp.swapaxes(acc[...], 0, 1)
    routed = jnp.concatenate([by_chunk[c] for c in range(ROW_TILES)], axis=1)
    out = _layer_norm_rows(s_ref[...] + routed, g_ref[...], b_ref[...])
    o32_ref[...] = out
    o16_ref[...] = out.astype(BF16)


def _combine_call(rows_tiles, pos, wts, cnt, start, off, s, g, b):
    t, d = s.shape
    nw = t // WINDOW
    smem_block = pl.BlockSpec((1, 1, PAIRS), lambda w, *_: (w, 0, 0), memory_space=pltpu.SMEM)
    row = pl.BlockSpec((WINDOW, d), lambda w, *_: (w, 0))
    vec = pl.BlockSpec((1, d), lambda w, *_: (0, 0))
    grid_spec = pltpu.PrefetchScalarGridSpec(
        num_scalar_prefetch=3,
        grid=(nw,),
        in_specs=[smem_block, smem_block, pl.BlockSpec(memory_space=pl.ANY), row, vec, vec],
        out_specs=[row, row],
        scratch_shapes=[pltpu.VMEM((2, PAIRS, ROW_TILES, LANES), BF16),
                        pltpu.VMEM((WINDOW, ROW_TILES, LANES), F32),
                        pltpu.SemaphoreType.DMA((2,))],
    )
    return pl.pallas_call(
        _combine_kernel,
        grid_spec=grid_spec,
        out_shape=[jax.ShapeDtypeStruct((t, d), F32), jax.ShapeDtypeStruct((t, d), BF16)],
        compiler_params=_params(("arbitrary",)),
        name="combine_ln",
    )(cnt, start, off, pos, wts, rows_tiles, s, g.reshape(1, d), b.reshape(1, d))


def _block_rows(first_block, n_blocks, first_row, rows_per_block, capacity):
    block = jnp.arange(capacity, dtype=I32)
    ends = first_block + n_blocks
    owner = jnp.minimum(jnp.sum((ends[None, :] <= block[:, None]).astype(I32), axis=1), N_EXPERTS - 1)
    return (first_row[owner] + (block - first_block[owner]) * rows_per_block).astype(I32)


def _row_layout(cnt, n_tokens):
    cnt2 = cnt.reshape(-1, N_EXPERTS)
    total = jnp.sum(cnt2, axis=0)
    padded = (total + EXPERT_CHUNK - 1) // EXPERT_CHUNK * EXPERT_CHUNK
    base = jnp.cumsum(padded) - padded
    off = base[None, :] + jnp.cumsum(cnt2, axis=0) - cnt2
    n_tile = padded // EXPERT_TILE
    n_chunk = (padded - n_tile * EXPERT_TILE) // EXPERT_CHUNK
    tile0 = jnp.cumsum(n_tile) - n_tile
    chunk0 = jnp.cumsum(n_chunk) - n_chunk
    tile_row = _block_rows(tile0, n_tile, base, EXPERT_TILE, n_tokens * TOP_K // EXPERT_TILE + N_EXPERTS)
    chunk_row = _block_rows(chunk0, n_chunk, base + n_tile * EXPERT_TILE, EXPERT_CHUNK,
                            N_EXPERTS * (EXPERT_TILE // EXPERT_CHUNK - 1))
    as_i32 = lambda *arrays: tuple(a.astype(I32) for a in arrays)
    return (as_i32(off.reshape(-1), base + total, padded - total),
            as_i32(tile0, n_tile, chunk0, n_chunk, tile_row, chunk_row))


def _moe_block(x32, x16, s, w_router, router_bias, w1_16, w3_16, w2a_16, w2b_16, g, b):
    t, d = x32.shape
    nw = t // WINDOW
    n_rows = t * TOP_K + N_EXPERTS * EXPERT_CHUNK
    wts, pos, cnt, start = _router_call(x32, w_router, router_bias)
    (off, pad_start, pad_len), blocks = _row_layout(cnt, t)
    cnt = cnt.reshape(-1)
    start = start.reshape(-1)
    pos = pos.reshape(nw, 1, PAIRS)
    wts = wts.reshape(nw, 1, PAIRS)
    rows = _dispatch_call(x16.reshape(t, ROW_TILES, LANES), pos, cnt, start, off, pad_start, pad_len, n_rows)
    out_rows = _expert_call(rows.reshape(n_rows * ROW_TILES, LANES), *blocks, w1_16, w3_16, w2a_16, w2b_16)
    return _combine_call(out_rows.reshape(n_rows, ROW_TILES, LANES), pos, wts, cnt, start, off, s, g, b)


def kernel(x, in_ln_g, in_ln_b, w_in, w_spatial, b_spatial, w_conv, w_pool, pool_scale,
           w_gate, b_gate, w_proj_a, w_proj_b, w_proj_c, w_out, ln1_g, ln1_b,
           w_router, router_bias, w_expert_gate, w_expert_up, w_expert_down,
           w_shared_gate, w_shared_up, w_shared_down, ln2_g, ln2_b):
    bsz, seq, d = x.shape
    x32, x16 = _ln_call(x.reshape(bsz * seq, d), in_ln_g, in_ln_b)
    for l in range(DEPTH):
        ya, yb, yc, w1_16, w2a_16 = _mix_tokens_call(
            x16, w_in[l].astype(BF16), w_spatial[l], b_spatial[l], w_conv[l], w_pool[l].astype(BF16),
            pool_scale[l], l, w_expert_gate, w_expert_down)
        merged, w3_16, w2b_16 = _merge_call(
            x16, ya, yb, yc, w_gate[l].astype(BF16), b_gate[l], w_proj_a[l].astype(BF16),
            w_proj_b[l].astype(BF16), w_proj_c[l].astype(BF16), l, w_expert_up, w_expert_down)
        x32, x16, s = _out_proj_call(merged, x32, w_out[l].astype(BF16), ln1_g[l], ln1_b[l],
                                     w_shared_gate[l].astype(BF16), w_shared_up[l].astype(BF16),
                                     w_shared_down[l].astype(BF16))
        x32, x16 = _moe_block(x32, x16, s, w_router[l], router_bias[l], w1_16, w3_16, w2a_16, w2b_16,
                              ln2_g[l], ln2_b[l])
    return x32.reshape(bsz, seq, d)
```

```python
import functools

import jax
import jax.numpy as jnp
from jax import lax
from jax.experimental import pallas as pl
from jax.experimental.pallas import tpu as pltpu

F32 = jnp.float32
BF16 = jnp.bfloat16
I32 = jnp.int32

D_MODEL = 2048
DEPTH = 2
CHUNK = 64
A_WIDTH = 1024
A_BLOCK = 128
N_HEAD_GROUPS = 4
HEAD = A_WIDTH // N_HEAD_GROUPS
N_STREAMS = 6
CONV_W = 3
N_BRANCH = 3
N_EXPERTS = 64
TOP_K = 8
N_GROUPS = 8
GROUP_SIZE = N_EXPERTS // N_GROUPS
TOPK_GROUPS = 4
ROUTED_SCALE = 2.5
ALPHA = (2.0 * DEPTH) ** 0.25
LN_EPS = 1e-5

LANES = 128
ROW_TILES = D_MODEL // LANES
MERGE_COLS = 512
WINDOW = 256
PAIRS = WINDOW * TOP_K
EXPERT_TILE = 512
EXPERT_CHUNK = 128
VMEM_LIMIT = 56 * 1024 * 1024

def _params(sem, vmem=VMEM_LIMIT):
    return pltpu.CompilerParams(dimension_semantics=sem, vmem_limit_bytes=vmem)


def _gelu(x):
    return 0.5 * x * (1.0 + lax.erf(x * (0.5 ** 0.5)))


def _layer_norm_rows(r, g, b):
    mu = jnp.mean(r, axis=-1, keepdims=True)
    c = r - mu
    var = jnp.mean(c * c, axis=-1, keepdims=True)
    return c * lax.rsqrt(var + LN_EPS) * g + b


def _ln_kernel(x_ref, g_ref, b_ref, o32_ref, o16_ref):
    y = _layer_norm_rows(x_ref[...], g_ref[...], b_ref[...])
    o32_ref[...] = y
    o16_ref[...] = y.astype(BF16)


def _ln_call(x, g, b, tm=256):
    t, d = x.shape
    row = pl.BlockSpec((tm, d), lambda i: (i, 0))
    vec = pl.BlockSpec((1, d), lambda i: (0, 0))
    return pl.pallas_call(
        _ln_kernel,
        grid=(t // tm,),
        in_specs=[row, vec, vec],
        out_specs=[row, row],
        out_shape=[jax.ShapeDtypeStruct((t, d), F32), jax.ShapeDtypeStruct((t, d), BF16)],
        compiler_params=_params(("parallel",)),
        name="layer_norm",
    )(x, g.reshape(1, d), b.reshape(1, d))


def _mix_tokens_kernel(x_ref, wu_ref, wv_ref, wbg_ref, wcg_ref, wh_ref, wpp_ref,
                       ws_ref, bs_ref, wc_ref, wp_ref, ps_ref, ew_ref, ed_ref,
                       ya_ref, yb_ref, yc_ref, ew16_ref, ed16_ref, zc_ref, pc_ref):
    g = pl.program_id(0)
    i = pl.program_id(1)
    tm = x_ref.shape[0]
    ew16_ref[...] = ew_ref[0].astype(BF16)
    ed16_ref[...] = ed_ref[0].astype(BF16)

    @pl.when(i == 0)
    def _():
        zc_ref[...] = jnp.zeros_like(zc_ref)
        pc_ref[...] = jnp.zeros_like(pc_ref)

    x = x_ref[...]
    proj = lambda w_ref: jnp.dot(x, w_ref[...], preferred_element_type=F32)

    u = _gelu(proj(wu_ref))
    v = _gelu(proj(wv_ref))
    mu = jnp.mean(v, axis=-1, keepdims=True)
    vc = v - mu
    var = jnp.mean(vc * vc, axis=-1, keepdims=True)
    vn = (vc * lax.rsqrt(var + LN_EPS)).astype(BF16)
    qi = lax.broadcasted_iota(I32, (A_BLOCK, A_BLOCK), 0) // CHUNK
    kj = lax.broadcasted_iota(I32, (A_BLOCK, A_BLOCK), 1) // CHUNK
    ws = jnp.where(kj <= qi, ws_ref[0], 0.0).astype(BF16)
    bs = bs_ref[0]
    for n in range(tm // A_BLOCK):
        rows = slice(n * A_BLOCK, (n + 1) * A_BLOCK)
        mixed = jnp.dot(ws, vn[rows], preferred_element_type=F32) + bs
        ya_ref[rows, :] = (u[rows] * mixed).astype(BF16)

    z = proj(wcg_ref) * proj(wh_ref)
    row = lax.broadcasted_iota(I32, (tm, 1), 0)
    prev1 = zc_ref[7:8, :]
    prev2 = zc_ref[6:7, :]
    z1 = jnp.where(row == 0, prev1, pltpu.roll(z, 1, 0))
    z2 = jnp.where(row == 0, prev2, jnp.where(row == 1, prev1, pltpu.roll(z, 2, 0)))
    conv = wc_ref[0:1, :] * z2 + wc_ref[1:2, :] * z1 + wc_ref[2:3, :] * z
    yb_ref[...] = (proj(wbg_ref) * conv).astype(BF16)
    zc_ref[...] = z[tm - 8:, :]

    p = proj(wpp_ref)
    win = lax.shift_left(jnp.int32(2), g)
    p16 = p.astype(BF16)
    pext = jnp.concatenate([pc_ref[...], p16], axis=0)
    rr = lax.broadcasted_iota(I32, (tm, tm + A_BLOCK), 0) + A_BLOCK
    cc = lax.broadcasted_iota(I32, (tm, tm + A_BLOCK), 1)
    band = jnp.where((cc <= rr) & (cc > rr - win), 1.0, 0.0).astype(BF16)
    total = jnp.dot(band, pext, preferred_element_type=F32)
    count = jnp.minimum(i * tm + row + 1, win).astype(F32)
    pooled = (total / count - p).astype(BF16)
    yc = jnp.dot(pooled, wp_ref[0], preferred_element_type=F32) * ps_ref[...]
    yc_ref[...] = yc.astype(BF16)
    pc_ref[...] = p16[tm - A_BLOCK:, :]


def _cast_stream_specs(w, layer, n_steps, step_of, halves=1, half=0):
    _, ne, r, c = w.shape
    per = ne // n_steps
    assert per * n_steps == ne and r % halves == 0
    rows = r // halves
    return (pl.BlockSpec((1, per, rows, c), lambda *idx: (layer, step_of(*idx), half, 0)),
            pl.BlockSpec((per, rows, c), lambda *idx: (step_of(*idx), 0, 0)),
            jax.ShapeDtypeStruct((ne, rows, c), BF16))


def _mix_tokens_call(xb, w_in16, w_s, b_s, w_conv, w_pool16, pool_scale, layer, expert_w, expert_down, tm=512):
    t, d = xb.shape
    nt = t // tm
    step_of = lambda g, i, *_: g * nt + i
    ew_in, ew_out, ew_shape = _cast_stream_specs(expert_w, layer, N_HEAD_GROUPS * nt, step_of)
    ed_in, ed_out, ed_shape = _cast_stream_specs(expert_down, layer, N_HEAD_GROUPS * nt, step_of, 2, 0)
    stream = lambda s: pl.BlockSpec((d, HEAD), lambda g, i: (0, s * N_HEAD_GROUPS + g))
    grp = lambda *shape: pl.BlockSpec((1,) + shape, lambda g, i: (g,) + (0,) * len(shape))
    out_spec = pl.BlockSpec((tm, HEAD), lambda g, i: (i, g))
    out_shape = jax.ShapeDtypeStruct((t, A_WIDTH), BF16)
    return pl.pallas_call(
        _mix_tokens_kernel,
        grid=(N_HEAD_GROUPS, t // tm),
        in_specs=[pl.BlockSpec((tm, d), lambda g, i: (i, 0))]
        + [stream(s) for s in range(N_STREAMS)]
        + [grp(A_BLOCK, A_BLOCK), grp(A_BLOCK, 1),
           pl.BlockSpec((CONV_W, HEAD), lambda g, i: (0, g)),
           grp(HEAD, HEAD),
           pl.BlockSpec((1, HEAD), lambda g, i: (0, g)),
           ew_in, ed_in],
        out_specs=[out_spec, out_spec, out_spec, ew_out, ed_out],
        out_shape=[out_shape, out_shape, out_shape, ew_shape, ed_shape],
        scratch_shapes=[pltpu.VMEM((8, HEAD), F32), pltpu.VMEM((A_BLOCK, HEAD), BF16)],
        compiler_params=_params(("arbitrary", "arbitrary")),
        name="mix_tokens",
    )(xb, *([w_in16] * N_STREAMS), w_s, b_s.reshape(N_HEAD_GROUPS, A_BLOCK, 1), w_conv,
      w_pool16, pool_scale.reshape(1, A_WIDTH), expert_w, expert_down)


def _merge_kernel(x_ref, ya_ref, yb_ref, yc_ref, wg0_ref, wg1_ref, wg2_ref, bg0_ref, bg1_ref, bg2_ref,
                  wa_ref, wb_ref, wc_ref, ew_ref, ed_ref, m_ref, ew16_ref, ed16_ref):
    ew16_ref[...] = ew_ref[0].astype(BF16)
    ed16_ref[...] = ed_ref[0].astype(BF16)
    x = x_ref[...]
    gate = lambda w_ref, b_ref: jax.nn.sigmoid(
        jnp.dot(x, w_ref[...], preferred_element_type=F32) + b_ref[...])
    m = gate(wg0_ref, bg0_ref) * jnp.dot(ya_ref[...], wa_ref[...], preferred_element_type=F32)
    m = m + gate(wg1_ref, bg1_ref) * jnp.dot(yb_ref[...], wb_ref[...], preferred_element_type=F32)
    m = m + gate(wg2_ref, bg2_ref) * jnp.dot(yc_ref[...], wc_ref[...], preferred_element_type=F32)
    m_ref[...] = m.astype(BF16)


def _merge_call(xb, ya, yb, yc, w_gate16, b_gate, wa16, wb16, wc16, layer, expert_w, expert_down, tm=512):
    t, d = xb.shape
    nj = d // MERGE_COLS
    nt = t // tm
    step_of = lambda j, i, *_: j * nt + i
    ew_in, ew_out, ew_shape = _cast_stream_specs(expert_w, layer, nj * nt, step_of)
    ed_in, ed_out, ed_shape = _cast_stream_specs(expert_down, layer, nj * nt, step_of, 2, 1)
    tok = lambda width: pl.BlockSpec((tm, width), lambda j, i: (i, 0))
    gate_w = lambda br: pl.BlockSpec((d, MERGE_COLS), lambda j, i: (0, br * nj + j))
    gate_b = lambda br: pl.BlockSpec((1, MERGE_COLS), lambda j, i: (0, br * nj + j))
    branch_w = pl.BlockSpec((A_WIDTH, MERGE_COLS), lambda j, i: (0, j))
    b2 = b_gate.reshape(1, N_BRANCH * d)
    return pl.pallas_call(
        _merge_kernel,
        grid=(nj, t // tm),
        in_specs=[tok(d), tok(A_WIDTH), tok(A_WIDTH), tok(A_WIDTH)]
        + [gate_w(br) for br in range(N_BRANCH)] + [gate_b(br) for br in range(N_BRANCH)]
        + [branch_w] * N_BRANCH + [ew_in, ed_in],
        out_specs=[pl.BlockSpec((tm, MERGE_COLS), lambda j, i: (i, j)), ew_out, ed_out],
        out_shape=[jax.ShapeDtypeStruct((t, d), BF16), ew_shape, ed_shape],
        compiler_params=_params(("arbitrary", "arbitrary")),
        name="merge_branches",
    )(xb, ya, yb, yc, w_gate16, w_gate16, w_gate16, b2, b2, b2, wa16, wb16, wc16, expert_w, expert_down)


def _out_proj_kernel(m_ref, x_ref, w_ref, g_ref, b_ref, wg_ref, wu_ref, wd_ref, o32_ref, o16_ref, s_ref):
    r = ALPHA * x_ref[...] + jnp.dot(m_ref[...], w_ref[...], preferred_element_type=F32)
    y = _layer_norm_rows(r, g_ref[...], b_ref[...])
    yb = y.astype(BF16)
    o32_ref[...] = y
    o16_ref[...] = yb
    a = jnp.dot(yb, wg_ref[...], preferred_element_type=F32)
    b = jnp.dot(yb, wu_ref[...], preferred_element_type=F32)
    hid = (jax.nn.silu(a) * b).astype(BF16)
    s_ref[...] = ALPHA * y + jnp.dot(hid, wd_ref[...], preferred_element_type=F32)


def _out_proj_call(merged, x, w_out16, g, b, wg16, wu16, wd16, tm=512):
    t, d = x.shape
    hdim = wg16.shape[1]
    row = pl.BlockSpec((tm, d), lambda i: (i, 0))
    vec = pl.BlockSpec((1, d), lambda i: (0, 0))
    whole = lambda r, c: pl.BlockSpec((r, c), lambda i: (0, 0), pipeline_mode=pl.Buffered(1))
    return pl.pallas_call(
        _out_proj_kernel,
        grid=(t // tm,),
        in_specs=[row, row, whole(d, d), vec, vec, whole(d, hdim), whole(d, hdim), whole(hdim, d)],
        out_specs=[row, row, row],
        out_shape=[jax.ShapeDtypeStruct((t, d), F32), jax.ShapeDtypeStruct((t, d), BF16),
                   jax.ShapeDtypeStruct((t, d), F32)],
        compiler_params=_params(("parallel",)),
        name="out_proj_ln_shared",
    )(merged, x, w_out16, g.reshape(1, d), b.reshape(1, d), wg16, wu16, wd16)


def _dot_nt(a, b):
    return lax.dot_general(a, b, (((1,), (1,)), ((), ())), preferred_element_type=F32)


def _split_bf16(v):
    hi = v.astype(BF16)
    return hi, (v - hi.astype(F32)).astype(BF16)


def _to_groups(v):
    return jnp.stack([v[g * GROUP_SIZE:(g + 1) * GROUP_SIZE] for g in range(N_GROUPS)])


def _from_groups(v):
    return jnp.concatenate([v[g] for g in range(N_GROUPS)], axis=0)


def _over_experts(fn, v):
    return fn(fn(v, axis=0, keepdims=True), axis=1, keepdims=True)


def _router_kernel(x_ref, wt_ref, b_ref, wts_ref, pos_ref, cnt_ref, start_ref):
    tm = x_ref.shape[0]
    neg = -jnp.inf
    xh, xl = _split_bf16(x_ref[...])
    wh, wl = _split_bf16(wt_ref[...])
    logits = _dot_nt(wh, xh) + (_dot_nt(wh, xl) + _dot_nt(wl, xh))
    scores2d = jax.nn.sigmoid(logits)
    scores = _to_groups(scores2d)
    sel = _to_groups(scores2d + b_ref[...])
    shape3 = (N_GROUPS, GROUP_SIZE, tm)
    member = lax.broadcasted_iota(I32, shape3, 1)
    expert = lax.broadcasted_iota(I32, shape3, 0) * GROUP_SIZE + member

    m1 = jnp.max(sel, axis=1, keepdims=True)
    i1 = jnp.min(jnp.where(sel == m1, member, GROUP_SIZE), axis=1, keepdims=True)
    m2 = jnp.max(jnp.where(member == i1, neg, sel), axis=1, keepdims=True)
    gscore = m1 + m2

    gid = lax.broadcasted_iota(I32, (N_GROUPS, 1, tm), 0)
    keep = jnp.zeros((N_GROUPS, 1, tm), jnp.bool_)
    for _ in range(TOPK_GROUPS):
        m = jnp.max(gscore, axis=0, keepdims=True)
        gsel = jnp.min(jnp.where(gscore == m, gid, N_GROUPS), axis=0, keepdims=True)
        hit = gid == gsel
        keep = keep | hit
        gscore = jnp.where(hit, neg, gscore)

    cand = jnp.where(keep, sel, neg)
    chosen, weight = [], []
    for _ in range(TOP_K):
        m = _over_experts(jnp.max, cand)
        idx = _over_experts(jnp.min, jnp.where(cand == m, expert, N_EXPERTS))
        hit = expert == idx
        chosen.append(idx)
        weight.append(_over_experts(jnp.sum, jnp.where(hit, scores, 0.0))[0])
        cand = jnp.where(hit, neg, cand)
    norm = functools.reduce(jnp.add, weight)
    wts_ref[0] = jnp.concatenate(weight, axis=0) / norm * ROUTED_SCALE

    onehot = functools.reduce(jnp.add, [jnp.where(expert == idx, 1.0, 0.0) for idx in chosen])
    onehot = _from_groups(onehot)
    t_r = lax.broadcasted_iota(I32, (tm, tm), 0)
    t_c = lax.broadcasted_iota(I32, (tm, tm), 1)
    earlier = jnp.where(t_r < t_c, 1.0, 0.0).astype(BF16)
    rank = jnp.dot(onehot.astype(BF16), earlier, preferred_element_type=F32)
    count = jnp.sum(onehot, axis=1, keepdims=True)
    e_r = lax.broadcasted_iota(I32, (N_EXPERTS, N_EXPERTS), 0)
    e_c = lax.broadcasted_iota(I32, (N_EXPERTS, N_EXPERTS), 1)
    before = jnp.where(e_c < e_r, 1.0, 0.0).astype(BF16)
    c_hi, c_lo = _split_bf16(jnp.broadcast_to(count, (N_EXPERTS, tm)))
    start = (jnp.dot(before, c_hi, preferred_element_type=F32)
             + jnp.dot(before, c_lo, preferred_element_type=F32))
    where_to = _to_groups(rank + start)
    pos = [_over_experts(jnp.sum, jnp.where(expert == idx, where_to, 0.0))[0] for idx in chosen]
    pos_ref[0] = jnp.concatenate(pos, axis=0).astype(I32)
    cnt_ref[0] = count.astype(I32)
    start_ref[0] = start[:, 0:1].astype(I32)


def _router_call(x, w_router, router_bias):
    t, d = x.shape
    nw = t // WINDOW
    per_tok = pl.BlockSpec((1, TOP_K, WINDOW), lambda i: (i, 0, 0))
    per_win = pl.BlockSpec((1, N_EXPERTS, 1), lambda i: (i, 0, 0))
    return pl.pallas_call(
        _router_kernel,
        grid=(nw,),
        in_specs=[pl.BlockSpec((WINDOW, d), lambda i: (i, 0)),
                  pl.BlockSpec((N_EXPERTS, d), lambda i: (0, 0)),
                  pl.BlockSpec((N_EXPERTS, 1), lambda i: (0, 0))],
        out_specs=[per_tok, per_tok, per_win, per_win],
        out_shape=[jax.ShapeDtypeStruct((nw, TOP_K, WINDOW), F32),
                   jax.ShapeDtypeStruct((nw, TOP_K, WINDOW), I32),
                   jax.ShapeDtypeStruct((nw, N_EXPERTS, 1), I32),
                   jax.ShapeDtypeStruct((nw, N_EXPERTS, 1), I32)],
        compiler_params=_params(("parallel",)),
        name="router",
    )(x, w_router.T, router_bias.reshape(N_EXPERTS, 1))


def _segment_copies(cnt_ref, start_ref, off_ref, win, local, remote, sem, to_remote, act):
    def body(e, carry):
        n = cnt_ref[win * N_EXPERTS + e]

        @pl.when(n > 0)
        def _():
            loc = local.at[pl.ds(start_ref[win * N_EXPERTS + e], n)]
            rem = remote.at[pl.ds(off_ref[win * N_EXPERTS + e], n)]
            act(pltpu.make_async_copy(loc, rem, sem) if to_remote
                else pltpu.make_async_copy(rem, loc, sem))
        return carry
    lax.fori_loop(0, N_EXPERTS, body, 0, unroll=4)


def _wait_window(local, remote, sem, to_remote):
    span = remote.at[pl.ds(0, PAIRS)]
    (pltpu.make_async_copy(local, span, sem) if to_remote else pltpu.make_async_copy(span, local, sem)).wait()


def _pad_copies(pad_start_ref, pad_len_ref, zeros, remote, sem, act):
    def body(e, carry):
        n = pad_len_ref[e]

        @pl.when(n > 0)
        def _():
            act(pltpu.make_async_copy(zeros.at[pl.ds(0, n)], remote.at[pl.ds(pad_start_ref[e], n)], sem))
        return carry
    lax.fori_loop(0, N_EXPERTS, body, 0)


def _dispatch_kernel(cnt_ref, start_ref, off_ref, pad_start_ref, pad_len_ref,
                     pos_ref, x_ref, rows_hbm, buf, zeros, sem, pad_sem):
    w = pl.program_id(0)
    last = pl.num_programs(0) - 1
    slot = w % 2
    start = lambda c: c.start()
    wait = lambda c: c.wait()
    copies = functools.partial(_segment_copies, cnt_ref, start_ref, off_ref,
                               remote=rows_hbm, to_remote=True)

    @pl.when(w == 0)
    def _():
        zeros[...] = jnp.zeros_like(zeros)
        _pad_copies(pad_start_ref, pad_len_ref, zeros, rows_hbm, pad_sem, start)

    @pl.when(w >= 2)
    def _():
        _wait_window(buf.at[slot], rows_hbm, sem.at[slot], True)

    def place(t, carry):
        v = x_ref[t]
        for k in range(TOP_K):
            buf[slot, pos_ref[0, 0, k * WINDOW + t]] = v
        return carry
    lax.fori_loop(0, WINDOW, place, 0, unroll=8)
    copies(win=w, local=buf.at[slot], sem=sem.at[slot], act=start)

    @pl.when(w == last)
    def _():
        @pl.when(w >= 1)
        def _():
            _wait_window(buf.at[1 - slot], rows_hbm, sem.at[1 - slot], True)
        _wait_window(buf.at[slot], rows_hbm, sem.at[slot], True)
        _pad_copies(pad_start_ref, pad_len_ref, zeros, rows_hbm, pad_sem, wait)


def _dispatch_cast_kernel(cnt_ref, start_ref, off_ref, pad_start_ref, pad_len_ref,
                          pos_ref, x_ref, ew_ref, rows_hbm, ew16_ref, *scratch):
    ew16_ref[...] = ew_ref[0].astype(BF16)
    _dispatch_kernel(cnt_ref, start_ref, off_ref, pad_start_ref, pad_len_ref, pos_ref, x_ref, rows_hbm, *scratch)


def _next_layer_stream(weight, layer, n_steps):
    if layer >= weight.shape[0]:
        return None
    n_layers, r, c = weight.shape
    w4 = weight.reshape(n_layers, n_steps, r // n_steps, c)
    return (w4,) + _cast_stream_specs(w4, layer, n_steps, lambda w, *_: w)


def _dispatch_call(x_tiles, pos, cnt, start, off, pad_start, pad_len, n_rows, stream):
    t = x_tiles.shape[0]
    nw = t // WINDOW
    rows_shape = jax.ShapeDtypeStruct((n_rows, ROW_TILES, LANES), BF16)
    in_specs = [pl.BlockSpec((1, 1, PAIRS), lambda w, *_: (w, 0, 0), memory_space=pltpu.SMEM),
                pl.BlockSpec((WINDOW, ROW_TILES, LANES), lambda w, *_: (w, 0, 0))]
    out_specs, out_shape, extra = pl.BlockSpec(memory_space=pl.ANY), rows_shape, ()
    if stream is not None:
        w4, ew_in, ew_out, ew_shape = stream
        in_specs, out_specs, out_shape, extra = in_specs + [ew_in], [out_specs, ew_out], [rows_shape, ew_shape], (w4,)
    grid_spec = pltpu.PrefetchScalarGridSpec(
        num_scalar_prefetch=5,
        grid=(nw,),
        in_specs=in_specs,
        out_specs=out_specs,
        scratch_shapes=[pltpu.VMEM((2, PAIRS, ROW_TILES, LANES), BF16),
                        pltpu.VMEM((EXPERT_CHUNK, ROW_TILES, LANES), BF16),
                        pltpu.SemaphoreType.DMA((2,)),
                        pltpu.SemaphoreType.DMA(())],
    )
    out = pl.pallas_call(
        _dispatch_kernel if stream is None else _dispatch_cast_kernel,
        grid_spec=grid_spec,
        out_shape=out_shape,
        compiler_params=_params(("arbitrary",)),
        name="dispatch",
    )(cnt, start, off, pad_start, pad_len, pos, x_tiles, *extra)
    return (out, None) if stream is None else out


class _RowStream:
    def __init__(self, tr, rows_hbm, out_hbm, weights, row_ref, total, xin, yout, in_sem, out_sem):
        self.tr, self.rows_hbm, self.out_hbm, self.weights = tr, rows_hbm, out_hbm, weights
        self.row_ref, self.total = row_ref, total
        self.xin, self.yout, self.in_sem, self.out_sem = xin, yout, in_sem, out_sem

    def _rows(self, i):
        first = pl.multiple_of(self.row_ref[i] * ROW_TILES, EXPERT_CHUNK * ROW_TILES)
        return pl.ds(first, self.tr * ROW_TILES)

    def _in_copy(self, i, slot):
        return pltpu.make_async_copy(self.rows_hbm.at[self._rows(i)], self.xin.at[slot], self.in_sem.at[slot])

    def _out_copy(self, i, slot):
        return pltpu.make_async_copy(self.yout.at[slot], self.out_hbm.at[self._rows(i)], self.out_sem.at[slot])

    def prime(self):
        @pl.when(self.total > 0)
        def _():
            self._in_copy(0, 0).start()

    def _block(self, i):
        tr, xin, yout = self.tr, self.xin, self.yout
        w1_ref, w3_ref, w2a_ref, w2b_ref = self.weights
        slot = i % 2
        self._in_copy(i, slot).wait()

        @pl.when(i + 1 < self.total)
        def _():
            self._in_copy(i + 1, 1 - slot).start()

        @pl.when(i >= 2)
        def _():
            self._out_copy(i - 2, slot).wait()

        x3 = jnp.swapaxes(xin[slot].reshape(tr, ROW_TILES, LANES), 0, 1)
        x = jnp.concatenate([x3[c] for c in range(ROW_TILES)], axis=1)
        a = jnp.dot(x, w1_ref[0], preferred_element_type=F32)
        b = jnp.dot(x, w3_ref[0], preferred_element_type=F32)
        hid = (jax.nn.silu(a) * b).astype(BF16)
        half = w2a_ref.shape[1]
        y = (jnp.dot(hid[:, :half], w2a_ref[0], preferred_element_type=F32)
             + jnp.dot(hid[:, half:], w2b_ref[0], preferred_element_type=F32))
        y3 = jnp.stack([y[:, c * LANES:(c + 1) * LANES].astype(BF16) for c in range(ROW_TILES)])
        yout[slot] = jnp.swapaxes(y3, 0, 1).reshape(tr * ROW_TILES, LANES)
        self._out_copy(i, slot).start()

    def run(self, first, count):
        def body(j, carry):
            self._block(first + j)
            return carry
        lax.fori_loop(0, count, body, 0)

    def drain(self):
        total = self.total

        @pl.when(total >= 2)
        def _():
            self._out_copy(total - 2, total % 2).wait()

        @pl.when(total >= 1)
        def _():
            self._out_copy(total - 1, (total - 1) % 2).wait()


def _expert_kernel(tile0_ref, n_tile_ref, chunk0_ref, n_chunk_ref, tile_row_ref, chunk_row_ref,
                   rows_hbm, w1_ref, w3_ref, w2a_ref, w2b_ref, out_hbm,
                   xin_t, yout_t, in_sem_t, out_sem_t, xin_c, yout_c, in_sem_c, out_sem_c):
    e = pl.program_id(0)
    last = pl.num_programs(0) - 1
    weights = (w1_ref, w3_ref, w2a_ref, w2b_ref)
    tiles = _RowStream(EXPERT_TILE, rows_hbm, out_hbm, weights, tile_row_ref,
                       tile0_ref[last] + n_tile_ref[last], xin_t, yout_t, in_sem_t, out_sem_t)
    chunks = _RowStream(EXPERT_CHUNK, rows_hbm, out_hbm, weights, chunk_row_ref,
                        chunk0_ref[last] + n_chunk_ref[last], xin_c, yout_c, in_sem_c, out_sem_c)

    @pl.when(e == 0)
    def _():
        tiles.prime()
        chunks.prime()

    tiles.run(tile0_ref[e], n_tile_ref[e])
    chunks.run(chunk0_ref[e], n_chunk_ref[e])

    @pl.when(e == last)
    def _():
        tiles.drain()
        chunks.drain()


def _expert_call(rows2d, tile0, n_tile, chunk0, n_chunk, tile_row, chunk_row, w1, w3, w2a, w2b):
    ne, d, hdim = w1.shape
    per_expert = lambda w: pl.BlockSpec((1,) + w.shape[1:], lambda e, *_: (e, 0, 0))
    stream_scratch = lambda tr: [pltpu.VMEM((2, tr * ROW_TILES, LANES), BF16),
                                 pltpu.VMEM((2, tr * ROW_TILES, LANES), BF16),
                                 pltpu.SemaphoreType.DMA((2,)), pltpu.SemaphoreType.DMA((2,))]
    grid_spec = pltpu.PrefetchScalarGridSpec(
        num_scalar_prefetch=6,
        grid=(ne,),
        in_specs=[pl.BlockSpec(memory_space=pl.ANY),
                  per_expert(w1), per_expert(w3), per_expert(w2a), per_expert(w2b)],
        out_specs=pl.BlockSpec(memory_space=pl.ANY),
        scratch_shapes=stream_scratch(EXPERT_TILE) + stream_scratch(EXPERT_CHUNK),
    )
    return pl.pallas_call(
        _expert_kernel,
        grid_spec=grid_spec,
        out_shape=jax.ShapeDtypeStruct(rows2d.shape, BF16),
        compiler_params=_params(("arbitrary",)),
        name="routed_experts",
    )(tile0, n_tile, chunk0, n_chunk, tile_row, chunk_row, rows2d, w1, w3, w2a, w2b)


def _combine_kernel(cnt_ref, start_ref, off_ref, pos_ref, wts_ref, rows_hbm, s_ref, g_ref, b_ref,
                    o32_ref, o16_ref, buf, acc, sem):
    w = pl.program_id(0)
    last = pl.num_programs(0) - 1
    slot = w % 2
    start = lambda c: c.start()
    wait = lambda c: c.wait()
    copies = functools.partial(_segment_copies, cnt_ref, start_ref, off_ref,
                               remote=rows_hbm, to_remote=False)

    @pl.when(w == 0)
    def _():
        copies(win=w, local=buf.at[slot], sem=sem.at[slot], act=start)

    @pl.when(w < last)
    def _():
        copies(win=w + 1, local=buf.at[1 - slot], sem=sem.at[1 - slot], act=start)

    _wait_window(buf.at[slot], rows_hbm, sem.at[slot], False)

    def token(t, carry):
        total = wts_ref[0, 0, t] * buf[slot, pos_ref[0, 0, t]].astype(F32)
        for k in range(1, TOP_K):
            total = total + wts_ref[0, 0, k * WINDOW + t] * buf[slot, pos_ref[0, 0, k * WINDOW + t]].astype(F32)
        acc[t] = total
        return carry
    lax.fori_loop(0, WINDOW, token, 0, unroll=8)

    by_chunk = jnp.swapaxes(acc[...], 0, 1)
    routed = jnp.concatenate([by_chunk[c] for c in range(ROW_TILES)], axis=1)
    out = _layer_norm_rows(s_ref[...] + routed, g_ref[...], b_ref[...])
    o32_ref[...] = out
    o16_ref[...] = out.astype(BF16)


def _combine_cast_kernel(cnt_ref, start_ref, off_ref, pos_ref, wts_ref, rows_hbm, s_ref, g_ref, b_ref, ew_ref,
                         o32_ref, o16_ref, ew16_ref, *scratch):
    ew16_ref[...] = ew_ref[0].astype(BF16)
    _combine_kernel(cnt_ref, start_ref, off_ref, pos_ref, wts_ref, rows_hbm, s_ref, g_ref, b_ref,
                    o32_ref, o16_ref, *scratch)


def _combine_call(rows_tiles, pos, wts, cnt, start, off, s, g, b, stream):
    t, d = s.shape
    nw = t // WINDOW
    smem_block = pl.BlockSpec((1, 1, PAIRS), lambda w, *_: (w, 0, 0), memory_space=pltpu.SMEM)
    row = pl.BlockSpec((WINDOW, d), lambda w, *_: (w, 0))
    vec = pl.BlockSpec((1, d), lambda w, *_: (0, 0))
    in_specs = [smem_block, smem_block, pl.BlockSpec(memory_space=pl.ANY), row, vec, vec]
    out_specs = [row, row]
    out_shape = [jax.ShapeDtypeStruct((t, d), F32), jax.ShapeDtypeStruct((t, d), BF16)]
    extra = ()
    if stream is not None:
        w4, ew_in, ew_out, ew_shape = stream
        in_specs, out_specs, out_shape, extra = in_specs + [ew_in], out_specs + [ew_out], out_shape + [ew_shape], (w4,)
    grid_spec = pltpu.PrefetchScalarGridSpec(
        num_scalar_prefetch=3,
        grid=(nw,),
        in_specs=in_specs,
        out_specs=out_specs,
        scratch_shapes=[pltpu.VMEM((2, PAIRS, ROW_TILES, LANES), BF16),
                        pltpu.VMEM((WINDOW, ROW_TILES, LANES), F32),
                        pltpu.SemaphoreType.DMA((2,))],
    )
    out = pl.pallas_call(
        _combine_kernel if stream is None else _combine_cast_kernel,
        grid_spec=grid_spec,
        out_shape=out_shape,
        compiler_params=_params(("arbitrary",)),
        name="combine_ln",
    )(cnt, start, off, pos, wts, rows_tiles, s, g.reshape(1, d), b.reshape(1, d), *extra)
    return (*out, None) if stream is None else out


def _block_rows(first_block, n_blocks, first_row, rows_per_block, capacity):
    block = jnp.arange(capacity, dtype=I32)
    ends = first_block + n_blocks
    owner = jnp.minimum(jnp.sum((ends[None, :] <= block[:, None]).astype(I32), axis=1), N_EXPERTS - 1)
    return (first_row[owner] + (block - first_block[owner]) * rows_per_block).astype(I32)


def _row_layout(cnt, n_tokens):
    cnt2 = cnt.reshape(-1, N_EXPERTS)
    total = jnp.sum(cnt2, axis=0)
    padded = (total + EXPERT_CHUNK - 1) // EXPERT_CHUNK * EXPERT_CHUNK
    base = jnp.cumsum(padded) - padded
    off = base[None, :] + jnp.cumsum(cnt2, axis=0) - cnt2
    n_tile = padded // EXPERT_TILE
    n_chunk = (padded - n_tile * EXPERT_TILE) // EXPERT_CHUNK
    tile0 = jnp.cumsum(n_tile) - n_tile
    chunk0 = jnp.cumsum(n_chunk) - n_chunk
    tile_row = _block_rows(tile0, n_tile, base, EXPERT_TILE, n_tokens * TOP_K // EXPERT_TILE + N_EXPERTS)
    chunk_row = _block_rows(chunk0, n_chunk, base + n_tile * EXPERT_TILE, EXPERT_CHUNK,
                            N_EXPERTS * (EXPERT_TILE // EXPERT_CHUNK - 1))
    as_i32 = lambda *arrays: tuple(a.astype(I32) for a in arrays)
    return (as_i32(off.reshape(-1), base + total, padded - total),
            as_i32(tile0, n_tile, chunk0, n_chunk, tile_row, chunk_row))


def _moe_block(x32, x16, s, w_router, router_bias, w1_16, w3_16, w2a_16, w2b_16, g, b, next_layer, w_in, w_gate):
    t, d = x32.shape
    nw = t // WINDOW
    n_rows = t * TOP_K + N_EXPERTS * EXPERT_CHUNK
    wts, pos, cnt, start = _router_call(x32, w_router, router_bias)
    (off, pad_start, pad_len), blocks = _row_layout(cnt, t)
    cnt = cnt.reshape(-1)
    start = start.reshape(-1)
    pos = pos.reshape(nw, 1, PAIRS)
    wts = wts.reshape(nw, 1, PAIRS)
    rows, w_in16 = _dispatch_call(x16.reshape(t, ROW_TILES, LANES), pos, cnt, start, off, pad_start, pad_len,
                                  n_rows, _next_layer_stream(w_in, next_layer, nw))
    out_rows = _expert_call(rows.reshape(n_rows * ROW_TILES, LANES), *blocks, w1_16, w3_16, w2a_16, w2b_16)
    y32, y16, w_gate16 = _combine_call(out_rows.reshape(n_rows, ROW_TILES, LANES), pos, wts, cnt, start, off,
                                       s, g, b, _next_layer_stream(w_gate, next_layer, nw))
    return y32, y16, w_in16, w_gate16


def kernel(x, in_ln_g, in_ln_b, w_in, w_spatial, b_spatial, w_conv, w_pool, pool_scale,
           w_gate, b_gate, w_proj_a, w_proj_b, w_proj_c, w_out, ln1_g, ln1_b,
           w_router, router_bias, w_expert_gate, w_expert_up, w_expert_down,
           w_shared_gate, w_shared_up, w_shared_down, ln2_g, ln2_b):
    bsz, seq, d = x.shape
    x32, x16 = _ln_call(x.reshape(bsz * seq, d), in_ln_g, in_ln_b)
    w_in16, w_gate16 = w_in[0].astype(BF16), w_gate[0].astype(BF16)
    for l in range(DEPTH):
        ya, yb, yc, w1_16, w2a_16 = _mix_tokens_call(
            x16, w_in16, w_spatial[l], b_spatial[l], w_conv[l], w_pool[l].astype(BF16),
            pool_scale[l], l, w_expert_gate, w_expert_down)
        merged, w3_16, w2b_16 = _merge_call(
            x16, ya, yb, yc, w_gate16, b_gate[l], w_proj_a[l].astype(BF16),
            w_proj_b[l].astype(BF16), w_proj_c[l].astype(BF16), l, w_expert_up, w_expert_down)
        x32, x16, s = _out_proj_call(merged, x32, w_out[l].astype(BF16), ln1_g[l], ln1_b[l],
                                     w_shared_gate[l].astype(BF16), w_shared_up[l].astype(BF16),
                                     w_shared_down[l].astype(BF16))
        x32, x16, w_in16, w_gate16 = _moe_block(
            x32, x16, s, w_router[l], router_bias[l], w1_16, w3_16, w2a_16, w2b_16, ln2_g[l], ln2_b[l],
            l + 1, w_in, w_gate)
        if w_in16 is not None:
            w_in16, w_gate16 = w_in16.reshape(w_in.shape[1:]), w_gate16.reshape(w_gate.shape[1:])
    return x32.reshape(bsz, seq, d)
```

```python
import functools

import jax
import jax.numpy as jnp
from jax import lax
from jax.experimental import pallas as pl
from jax.experimental.pallas import tpu as pltpu

F32 = jnp.float32
BF16 = jnp.bfloat16
I32 = jnp.int32

D_MODEL = 2048
DEPTH = 2
CHUNK = 64
A_WIDTH = 1024
A_BLOCK = 128
N_HEAD_GROUPS = 4
HEAD = A_WIDTH // N_HEAD_GROUPS
N_STREAMS = 6
CONV_W = 3
N_BRANCH = 3
N_EXPERTS = 64
TOP_K = 8
N_GROUPS = 8
GROUP_SIZE = N_EXPERTS // N_GROUPS
TOPK_GROUPS = 4
ROUTED_SCALE = 2.5
ALPHA = (2.0 * DEPTH) ** 0.25
LN_EPS = 1e-5

LANES = 128
ROW_TILES = D_MODEL // LANES
MERGE_COLS = 512
WINDOW = 256
PAIRS = WINDOW * TOP_K
EXPERT_TILE = 512
EXPERT_CHUNK = 128
VMEM_LIMIT = 56 * 1024 * 1024

def _params(sem, vmem=VMEM_LIMIT):
    return pltpu.CompilerParams(dimension_semantics=sem, vmem_limit_bytes=vmem)


def _gelu(x):
    return 0.5 * x * (1.0 + lax.erf(x * (0.5 ** 0.5)))


def _layer_norm_rows(r, g, b):
    mu = jnp.mean(r, axis=-1, keepdims=True)
    c = r - mu
    var = jnp.mean(c * c, axis=-1, keepdims=True)
    return c * lax.rsqrt(var + LN_EPS) * g + b


def _ln_kernel(x_ref, g_ref, b_ref, o32_ref, o16_ref):
    y = _layer_norm_rows(x_ref[...], g_ref[...], b_ref[...])
    o32_ref[...] = y
    o16_ref[...] = y.astype(BF16)


def _ln_call(x, g, b, tm=256):
    t, d = x.shape
    row = pl.BlockSpec((tm, d), lambda i: (i, 0))
    vec = pl.BlockSpec((1, d), lambda i: (0, 0))
    return pl.pallas_call(
        _ln_kernel,
        grid=(t // tm,),
        in_specs=[row, vec, vec],
        out_specs=[row, row],
        out_shape=[jax.ShapeDtypeStruct((t, d), F32), jax.ShapeDtypeStruct((t, d), BF16)],
        compiler_params=_params(("parallel",)),
        name="layer_norm",
    )(x, g.reshape(1, d), b.reshape(1, d))


def _mix_tokens_kernel(x_ref, wu_ref, wv_ref, wbg_ref, wcg_ref, wh_ref, wpp_ref,
                       ws_ref, bs_ref, wc_ref, wp_ref, ps_ref, ew_ref, ed_ref,
                       ya_ref, yb_ref, yc_ref, ew16_ref, ed16_ref, zc_ref, pc_ref):
    g = pl.program_id(0)
    i = pl.program_id(1)
    tm = x_ref.shape[0]
    ew16_ref[...] = ew_ref[0].astype(BF16)
    ed16_ref[...] = ed_ref[0].astype(BF16)

    @pl.when(i == 0)
    def _():
        zc_ref[...] = jnp.zeros_like(zc_ref)
        pc_ref[...] = jnp.zeros_like(pc_ref)

    x = x_ref[...]
    proj = lambda w_ref: jnp.dot(x, w_ref[...], preferred_element_type=F32)

    u = _gelu(proj(wu_ref))
    v = _gelu(proj(wv_ref))
    mu = jnp.mean(v, axis=-1, keepdims=True)
    vc = v - mu
    var = jnp.mean(vc * vc, axis=-1, keepdims=True)
    vn = (vc * lax.rsqrt(var + LN_EPS)).astype(BF16)
    qi = lax.broadcasted_iota(I32, (A_BLOCK, A_BLOCK), 0) // CHUNK
    kj = lax.broadcasted_iota(I32, (A_BLOCK, A_BLOCK), 1) // CHUNK
    ws = jnp.where(kj <= qi, ws_ref[0], 0.0).astype(BF16)
    bs = bs_ref[0]
    for n in range(tm // A_BLOCK):
        rows = slice(n * A_BLOCK, (n + 1) * A_BLOCK)
        mixed = jnp.dot(ws, vn[rows], preferred_element_type=F32) + bs
        ya_ref[rows, :] = (u[rows] * mixed).astype(BF16)

    z = proj(wcg_ref) * proj(wh_ref)
    row = lax.broadcasted_iota(I32, (tm, 1), 0)
    prev1 = zc_ref[7:8, :]
    prev2 = zc_ref[6:7, :]
    z1 = jnp.where(row == 0, prev1, pltpu.roll(z, 1, 0))
    z2 = jnp.where(row == 0, prev2, jnp.where(row == 1, prev1, pltpu.roll(z, 2, 0)))
    conv = wc_ref[0:1, :] * z2 + wc_ref[1:2, :] * z1 + wc_ref[2:3, :] * z
    yb_ref[...] = (proj(wbg_ref) * conv).astype(BF16)
    zc_ref[...] = z[tm - 8:, :]

    p = proj(wpp_ref)
    win = lax.shift_left(jnp.int32(2), g)
    p16 = p.astype(BF16)
    pext = jnp.concatenate([pc_ref[...], p16], axis=0)
    rr = lax.broadcasted_iota(I32, (tm, tm + A_BLOCK), 0) + A_BLOCK
    cc = lax.broadcasted_iota(I32, (tm, tm + A_BLOCK), 1)
    band = jnp.where((cc <= rr) & (cc > rr - win), 1.0, 0.0).astype(BF16)
    total = jnp.dot(band, pext, preferred_element_type=F32)
    count = jnp.minimum(i * tm + row + 1, win).astype(F32)
    pooled = (total / count - p).astype(BF16)
    yc = jnp.dot(pooled, wp_ref[0], preferred_element_type=F32) * ps_ref[...]
    yc_ref[...] = yc.astype(BF16)
    pc_ref[...] = p16[tm - A_BLOCK:, :]


def _cast_stream_specs(w, layer, n_steps, step_of, halves=1, half=0):
    _, ne, r, c = w.shape
    per = ne // n_steps
    assert per * n_steps == ne and r % halves == 0
    rows = r // halves
    return (pl.BlockSpec((1, per, rows, c), lambda *idx: (layer, step_of(*idx), half, 0)),
            pl.BlockSpec((per, rows, c), lambda *idx: (step_of(*idx), 0, 0)),
            jax.ShapeDtypeStruct((ne, rows, c), BF16))


def _mix_tokens_call(xb, w_in16, w_s, b_s, w_conv, w_pool16, pool_scale, layer, expert_w, expert_down, tm=512):
    t, d = xb.shape
    nt = t // tm
    step_of = lambda g, i, *_: g * nt + i
    ew_in, ew_out, ew_shape = _cast_stream_specs(expert_w, layer, N_HEAD_GROUPS * nt, step_of)
    ed_in, ed_out, ed_shape = _cast_stream_specs(expert_down, layer, N_HEAD_GROUPS * nt, step_of, 2, 0)
    stream = lambda s: pl.BlockSpec((d, HEAD), lambda g, i: (0, s * N_HEAD_GROUPS + g))
    grp = lambda *shape: pl.BlockSpec((1,) + shape, lambda g, i: (g,) + (0,) * len(shape))
    out_spec = pl.BlockSpec((tm, HEAD), lambda g, i: (i, g))
    out_shape = jax.ShapeDtypeStruct((t, A_WIDTH), BF16)
    return pl.pallas_call(
        _mix_tokens_kernel,
        grid=(N_HEAD_GROUPS, t // tm),
        in_specs=[pl.BlockSpec((tm, d), lambda g, i: (i, 0))]
        + [stream(s) for s in range(N_STREAMS)]
        + [grp(A_BLOCK, A_BLOCK), grp(A_BLOCK, 1),
           pl.BlockSpec((CONV_W, HEAD), lambda g, i: (0, g)),
           grp(HEAD, HEAD),
           pl.BlockSpec((1, HEAD), lambda g, i: (0, g)),
           ew_in, ed_in],
        out_specs=[out_spec, out_spec, out_spec, ew_out, ed_out],
        out_shape=[out_shape, out_shape, out_shape, ew_shape, ed_shape],
        scratch_shapes=[pltpu.VMEM((8, HEAD), F32), pltpu.VMEM((A_BLOCK, HEAD), BF16)],
        compiler_params=_params(("arbitrary", "arbitrary")),
        name="mix_tokens",
    )(xb, *([w_in16] * N_STREAMS), w_s, b_s.reshape(N_HEAD_GROUPS, A_BLOCK, 1), w_conv,
      w_pool16, pool_scale.reshape(1, A_WIDTH), expert_w, expert_down)


def _merge_kernel(x_ref, ya_ref, yb_ref, yc_ref, wg0_ref, wg1_ref, wg2_ref, bg0_ref, bg1_ref, bg2_ref,
                  wa_ref, wb_ref, wc_ref, ew_ref, ed_ref, m_ref, ew16_ref, ed16_ref):
    ew16_ref[...] = ew_ref[0].astype(BF16)
    ed16_ref[...] = ed_ref[0].astype(BF16)
    x = x_ref[...]
    gate = lambda w_ref, b_ref: jax.nn.sigmoid(
        jnp.dot(x, w_ref[...], preferred_element_type=F32) + b_ref[...])
    m = gate(wg0_ref, bg0_ref) * jnp.dot(ya_ref[...], wa_ref[...], preferred_element_type=F32)
    m = m + gate(wg1_ref, bg1_ref) * jnp.dot(yb_ref[...], wb_ref[...], preferred_element_type=F32)
    m = m + gate(wg2_ref, bg2_ref) * jnp.dot(yc_ref[...], wc_ref[...], preferred_element_type=F32)
    m_ref[...] = m.astype(BF16)


def _merge_call(xb, ya, yb, yc, w_gate16, b_gate, wa16, wb16, wc16, layer, expert_w, expert_down, tm=512):
    t, d = xb.shape
    nj = d // MERGE_COLS
    nt = t // tm
    step_of = lambda j, i, *_: j * nt + i
    ew_in, ew_out, ew_shape = _cast_stream_specs(expert_w, layer, nj * nt, step_of)
    ed_in, ed_out, ed_shape = _cast_stream_specs(expert_down, layer, nj * nt, step_of, 2, 1)
    tok = lambda width: pl.BlockSpec((tm, width), lambda j, i: (i, 0))
    gate_w = lambda br: pl.BlockSpec((d, MERGE_COLS), lambda j, i: (0, br * nj + j))
    gate_b = lambda br: pl.BlockSpec((1, MERGE_COLS), lambda j, i: (0, br * nj + j))
    branch_w = pl.BlockSpec((A_WIDTH, MERGE_COLS), lambda j, i: (0, j))
    b2 = b_gate.reshape(1, N_BRANCH * d)
    return pl.pallas_call(
        _merge_kernel,
        grid=(nj, t // tm),
        in_specs=[tok(d), tok(A_WIDTH), tok(A_WIDTH), tok(A_WIDTH)]
        + [gate_w(br) for br in range(N_BRANCH)] + [gate_b(br) for br in range(N_BRANCH)]
        + [branch_w] * N_BRANCH + [ew_in, ed_in],
        out_specs=[pl.BlockSpec((tm, MERGE_COLS), lambda j, i: (i, j)), ew_out, ed_out],
        out_shape=[jax.ShapeDtypeStruct((t, d), BF16), ew_shape, ed_shape],
        compiler_params=_params(("arbitrary", "arbitrary")),
        name="merge_branches",
    )(xb, ya, yb, yc, w_gate16, w_gate16, w_gate16, b2, b2, b2, wa16, wb16, wc16, expert_w, expert_down)


def _out_proj_kernel(m_ref, x_ref, w_ref, g_ref, b_ref, wg_ref, wu_ref, wd_ref, o32_ref, o16_ref, s_ref):
    r = ALPHA * x_ref[...] + jnp.dot(m_ref[...], w_ref[...], preferred_element_type=F32)
    y = _layer_norm_rows(r, g_ref[...], b_ref[...])
    yb = y.astype(BF16)
    o32_ref[...] = y
    o16_ref[...] = yb
    a = jnp.dot(yb, wg_ref[...], preferred_element_type=F32)
    b = jnp.dot(yb, wu_ref[...], preferred_element_type=F32)
    hid = (jax.nn.silu(a) * b).astype(BF16)
    s_ref[...] = ALPHA * y + jnp.dot(hid, wd_ref[...], preferred_element_type=F32)


def _out_proj_call(merged, x, w_out16, g, b, wg16, wu16, wd16, tm=512):
    t, d = x.shape
    hdim = wg16.shape[1]
    row = pl.BlockSpec((tm, d), lambda i: (i, 0))
    vec = pl.BlockSpec((1, d), lambda i: (0, 0))
    whole = lambda r, c: pl.BlockSpec((r, c), lambda i: (0, 0), pipeline_mode=pl.Buffered(1))
    return pl.pallas_call(
        _out_proj_kernel,
        grid=(t // tm,),
        in_specs=[row, row, whole(d, d), vec, vec, whole(d, hdim), whole(d, hdim), whole(hdim, d)],
        out_specs=[row, row, row],
        out_shape=[jax.ShapeDtypeStruct((t, d), F32), jax.ShapeDtypeStruct((t, d), BF16),
                   jax.ShapeDtypeStruct((t, d), F32)],
        compiler_params=_params(("parallel",)),
        name="out_proj_ln_shared",
    )(merged, x, w_out16, g.reshape(1, d), b.reshape(1, d), wg16, wu16, wd16)


def _dot_nt(a, b):
    return lax.dot_general(a, b, (((1,), (1,)), ((), ())), preferred_element_type=F32)


def _split_bf16(v):
    hi = v.astype(BF16)
    return hi, (v - hi.astype(F32)).astype(BF16)


def _to_groups(v):
    return jnp.stack([v[g * GROUP_SIZE:(g + 1) * GROUP_SIZE] for g in range(N_GROUPS)])


def _from_groups(v):
    return jnp.concatenate([v[g] for g in range(N_GROUPS)], axis=0)


def _over_experts(fn, v):
    return fn(fn(v, axis=0, keepdims=True), axis=1, keepdims=True)


def _router_kernel(x_ref, wt_ref, b_ref, wts_ref, pos_ref, cnt_ref, start_ref):
    tm = x_ref.shape[0]
    neg = -jnp.inf
    xh, xl = _split_bf16(x_ref[...])
    wh, wl = _split_bf16(wt_ref[...])
    logits = _dot_nt(wh, xh) + (_dot_nt(wh, xl) + _dot_nt(wl, xh))
    scores2d = jax.nn.sigmoid(logits)
    scores = _to_groups(scores2d)
    sel = _to_groups(scores2d + b_ref[...])
    shape3 = (N_GROUPS, GROUP_SIZE, tm)
    member = lax.broadcasted_iota(I32, shape3, 1)
    expert = lax.broadcasted_iota(I32, shape3, 0) * GROUP_SIZE + member

    m1 = jnp.max(sel, axis=1, keepdims=True)
    i1 = jnp.min(jnp.where(sel == m1, member, GROUP_SIZE), axis=1, keepdims=True)
    m2 = jnp.max(jnp.where(member == i1, neg, sel), axis=1, keepdims=True)
    gscore = m1 + m2

    gid = lax.broadcasted_iota(I32, (N_GROUPS, 1, tm), 0)
    keep = jnp.zeros((N_GROUPS, 1, tm), jnp.bool_)
    for _ in range(TOPK_GROUPS):
        m = jnp.max(gscore, axis=0, keepdims=True)
        gsel = jnp.min(jnp.where(gscore == m, gid, N_GROUPS), axis=0, keepdims=True)
        hit = gid == gsel
        keep = keep | hit
        gscore = jnp.where(hit, neg, gscore)

    cand = jnp.where(keep, sel, neg)
    chosen, weight = [], []
    for _ in range(TOP_K):
        m = _over_experts(jnp.max, cand)
        idx = _over_experts(jnp.min, jnp.where(cand == m, expert, N_EXPERTS))
        hit = expert == idx
        chosen.append(idx)
        weight.append(_over_experts(jnp.sum, jnp.where(hit, scores, 0.0))[0])
        cand = jnp.where(hit, neg, cand)
    norm = functools.reduce(jnp.add, weight)
    wts_ref[0] = jnp.concatenate(weight, axis=0) / norm * ROUTED_SCALE

    onehot = functools.reduce(jnp.add, [jnp.where(expert == idx, 1.0, 0.0) for idx in chosen])
    onehot = _from_groups(onehot)
    t_r = lax.broadcasted_iota(I32, (tm, tm), 0)
    t_c = lax.broadcasted_iota(I32, (tm, tm), 1)
    earlier = jnp.where(t_r < t_c, 1.0, 0.0).astype(BF16)
    rank = jnp.dot(onehot.astype(BF16), earlier, preferred_element_type=F32)
    count = jnp.sum(onehot, axis=1, keepdims=True)
    e_r = lax.broadcasted_iota(I32, (N_EXPERTS, N_EXPERTS), 0)
    e_c = lax.broadcasted_iota(I32, (N_EXPERTS, N_EXPERTS), 1)
    before = jnp.where(e_c < e_r, 1.0, 0.0).astype(BF16)
    c_hi, c_lo = _split_bf16(jnp.broadcast_to(count, (N_EXPERTS, tm)))
    start = (jnp.dot(before, c_hi, preferred_element_type=F32)
             + jnp.dot(before, c_lo, preferred_element_type=F32))
    where_to = _to_groups(rank + start)
    pos = [_over_experts(jnp.sum, jnp.where(expert == idx, where_to, 0.0))[0] for idx in chosen]
    pos_ref[0] = jnp.concatenate(pos, axis=0).astype(I32)
    cnt_ref[0] = count.astype(I32)
    start_ref[0] = start[:, 0:1].astype(I32)


def _router_call(x, w_router, router_bias):
    t, d = x.shape
    nw = t // WINDOW
    per_tok = pl.BlockSpec((1, TOP_K, WINDOW), lambda i: (i, 0, 0))
    per_win = pl.BlockSpec((1, N_EXPERTS, 1), lambda i: (i, 0, 0))
    return pl.pallas_call(
        _router_kernel,
        grid=(nw,),
        in_specs=[pl.BlockSpec((WINDOW, d), lambda i: (i, 0)),
                  pl.BlockSpec((N_EXPERTS, d), lambda i: (0, 0)),
                  pl.BlockSpec((N_EXPERTS, 1), lambda i: (0, 0))],
        out_specs=[per_tok, per_tok, per_win, per_win],
        out_shape=[jax.ShapeDtypeStruct((nw, TOP_K, WINDOW), F32),
                   jax.ShapeDtypeStruct((nw, TOP_K, WINDOW), I32),
                   jax.ShapeDtypeStruct((nw, N_EXPERTS, 1), I32),
                   jax.ShapeDtypeStruct((nw, N_EXPERTS, 1), I32)],
        compiler_params=_params(("parallel",)),
        name="router",
    )(x, w_router.T, router_bias.reshape(N_EXPERTS, 1))


def _segment_copies(cnt_ref, start_ref, off_ref, win, local, remote, sem, to_remote, act):
    def body(e, carry):
        n = cnt_ref[win * N_EXPERTS + e]

        @pl.when(n > 0)
        def _():
            loc = local.at[pl.ds(start_ref[win * N_EXPERTS + e], n)]
            rem = remote.at[pl.ds(off_ref[win * N_EXPERTS + e], n)]
            act(pltpu.make_async_copy(loc, rem, sem) if to_remote
                else pltpu.make_async_copy(rem, loc, sem))
        return carry
    lax.fori_loop(0, N_EXPERTS, body, 0, unroll=8)


def _wait_window(local, remote, sem, to_remote):
    span = remote.at[pl.ds(0, PAIRS)]
    (pltpu.make_async_copy(local, span, sem) if to_remote else pltpu.make_async_copy(span, local, sem)).wait()


def _pad_copies(pad_start_ref, pad_len_ref, zeros, remote, sem, act):
    def body(e, carry):
        n = pad_len_ref[e]

        @pl.when(n > 0)
        def _():
            act(pltpu.make_async_copy(zeros.at[pl.ds(0, n)], remote.at[pl.ds(pad_start_ref[e], n)], sem))
        return carry
    lax.fori_loop(0, N_EXPERTS, body, 0)


def _dispatch_kernel(cnt_ref, start_ref, off_ref, pad_start_ref, pad_len_ref,
                     pos_ref, x_ref, rows_hbm, buf, zeros, sem, pad_sem):
    w = pl.program_id(0)
    last = pl.num_programs(0) - 1
    slot = w % 2
    start = lambda c: c.start()
    wait = lambda c: c.wait()
    copies = functools.partial(_segment_copies, cnt_ref, start_ref, off_ref,
                               remote=rows_hbm, to_remote=True)

    @pl.when(w == 0)
    def _():
        zeros[...] = jnp.zeros_like(zeros)
        _pad_copies(pad_start_ref, pad_len_ref, zeros, rows_hbm, pad_sem, start)

    @pl.when(w >= 2)
    def _():
        _wait_window(buf.at[slot], rows_hbm, sem.at[slot], True)

    def place(t, carry):
        v = x_ref[t]
        for k in range(TOP_K):
            buf[slot, pos_ref[0, 0, k * WINDOW + t]] = v
        return carry
    lax.fori_loop(0, WINDOW, place, 0, unroll=16)
    copies(win=w, local=buf.at[slot], sem=sem.at[slot], act=start)

    @pl.when(w == last)
    def _():
        @pl.when(w >= 1)
        def _():
            _wait_window(buf.at[1 - slot], rows_hbm, sem.at[1 - slot], True)
        _wait_window(buf.at[slot], rows_hbm, sem.at[slot], True)
        _pad_copies(pad_start_ref, pad_len_ref, zeros, rows_hbm, pad_sem, wait)


def _dispatch_cast_kernel(cnt_ref, start_ref, off_ref, pad_start_ref, pad_len_ref,
                          pos_ref, x_ref, ew_ref, rows_hbm, ew16_ref, *scratch):
    ew16_ref[...] = ew_ref[0].astype(BF16)
    _dispatch_kernel(cnt_ref, start_ref, off_ref, pad_start_ref, pad_len_ref, pos_ref, x_ref, rows_hbm, *scratch)


def _next_layer_stream(weight, layer, n_steps):
    if layer >= weight.shape[0]:
        return None
    n_layers, r, c = weight.shape
    w4 = weight.reshape(n_layers, n_steps, r // n_steps, c)
    return (w4,) + _cast_stream_specs(w4, layer, n_steps, lambda w, *_: w)


def _dispatch_call(x_tiles, pos, cnt, start, off, pad_start, pad_len, n_rows, stream):
    t = x_tiles.shape[0]
    nw = t // WINDOW
    rows_shape = jax.ShapeDtypeStruct((n_rows, ROW_TILES, LANES), BF16)
    in_specs = [pl.BlockSpec((1, 1, PAIRS), lambda w, *_: (w, 0, 0), memory_space=pltpu.SMEM),
                pl.BlockSpec((WINDOW, ROW_TILES, LANES), lambda w, *_: (w, 0, 0))]
    out_specs, out_shape, extra = pl.BlockSpec(memory_space=pl.ANY), rows_shape, ()
    if stream is not None:
        w4, ew_in, ew_out, ew_shape = stream
        in_specs, out_specs, out_shape, extra = in_specs + [ew_in], [out_specs, ew_out], [rows_shape, ew_shape], (w4,)
    grid_spec = pltpu.PrefetchScalarGridSpec(
        num_scalar_prefetch=5,
        grid=(nw,),
        in_specs=in_specs,
        out_specs=out_specs,
        scratch_shapes=[pltpu.VMEM((2, PAIRS, ROW_TILES, LANES), BF16),
                        pltpu.VMEM((EXPERT_CHUNK, ROW_TILES, LANES), BF16),
                        pltpu.SemaphoreType.DMA((2,)),
                        pltpu.SemaphoreType.DMA(())],
    )
    out = pl.pallas_call(
        _dispatch_kernel if stream is None else _dispatch_cast_kernel,
        grid_spec=grid_spec,
        out_shape=out_shape,
        compiler_params=_params(("arbitrary",)),
        name="dispatch",
    )(cnt, start, off, pad_start, pad_len, pos, x_tiles, *extra)
    return (out, None) if stream is None else out


class _RowStream:
    def __init__(self, tr, rows_hbm, out_hbm, weights, row_ref, total, xin, yout, in_sem, out_sem):
        self.tr, self.rows_hbm, self.out_hbm, self.weights = tr, rows_hbm, out_hbm, weights
        self.row_ref, self.total = row_ref, total
        self.xin, self.yout, self.in_sem, self.out_sem = xin, yout, in_sem, out_sem

    def _rows(self, i):
        first = pl.multiple_of(self.row_ref[i] * ROW_TILES, EXPERT_CHUNK * ROW_TILES)
        return pl.ds(first, self.tr * ROW_TILES)

    def _in_copy(self, i, slot):
        return pltpu.make_async_copy(self.rows_hbm.at[self._rows(i)], self.xin.at[slot], self.in_sem.at[slot])

    def _out_copy(self, i, slot):
        return pltpu.make_async_copy(self.yout.at[slot], self.out_hbm.at[self._rows(i)], self.out_sem.at[slot])

    def prime(self):
        @pl.when(self.total > 0)
        def _():
            self._in_copy(0, 0).start()

    def _block(self, i):
        tr, xin, yout = self.tr, self.xin, self.yout
        w1_ref, w3_ref, w2a_ref, w2b_ref = self.weights
        slot = i % 2
        self._in_copy(i, slot).wait()

        @pl.when(i + 1 < self.total)
        def _():
            self._in_copy(i + 1, 1 - slot).start()

        @pl.when(i >= 2)
        def _():
            self._out_copy(i - 2, slot).wait()

        x3 = jnp.swapaxes(xin[slot].reshape(tr, ROW_TILES, LANES), 0, 1)
        x = jnp.concatenate([x3[c] for c in range(ROW_TILES)], axis=1)
        a = jnp.dot(x, w1_ref[0], preferred_element_type=F32)
        b = jnp.dot(x, w3_ref[0], preferred_element_type=F32)
        hid = (jax.nn.silu(a) * b).astype(BF16)
        half = w2a_ref.shape[1]
        y = (jnp.dot(hid[:, :half], w2a_ref[0], preferred_element_type=F32)
             + jnp.dot(hid[:, half:], w2b_ref[0], preferred_element_type=F32))
        y3 = jnp.stack([y[:, c * LANES:(c + 1) * LANES].astype(BF16) for c in range(ROW_TILES)])
        yout[slot] = jnp.swapaxes(y3, 0, 1).reshape(tr * ROW_TILES, LANES)
        self._out_copy(i, slot).start()

    def run(self, first, count):
        def body(j, carry):
            self._block(first + j)
            return carry
        lax.fori_loop(0, count, body, 0)

    def drain(self):
        total = self.total

        @pl.when(total >= 2)
        def _():
            self._out_copy(total - 2, total % 2).wait()

        @pl.when(total >= 1)
        def _():
            self._out_copy(total - 1, (total - 1) % 2).wait()


def _expert_kernel(tile0_ref, n_tile_ref, chunk0_ref, n_chunk_ref, tile_row_ref, chunk_row_ref,
                   rows_hbm, w1_ref, w3_ref, w2a_ref, w2b_ref, out_hbm,
                   xin_t, yout_t, in_sem_t, out_sem_t, xin_c, yout_c, in_sem_c, out_sem_c):
    e = pl.program_id(0)
    last = pl.num_programs(0) - 1
    weights = (w1_ref, w3_ref, w2a_ref, w2b_ref)
    tiles = _RowStream(EXPERT_TILE, rows_hbm, out_hbm, weights, tile_row_ref,
                       tile0_ref[last] + n_tile_ref[last], xin_t, yout_t, in_sem_t, out_sem_t)
    chunks = _RowStream(EXPERT_CHUNK, rows_hbm, out_hbm, weights, chunk_row_ref,
                        chunk0_ref[last] + n_chunk_ref[last], xin_c, yout_c, in_sem_c, out_sem_c)

    @pl.when(e == 0)
    def _():
        tiles.prime()
        chunks.prime()

    tiles.run(tile0_ref[e], n_tile_ref[e])
    chunks.run(chunk0_ref[e], n_chunk_ref[e])

    @pl.when(e == last)
    def _():
        tiles.drain()
        chunks.drain()


def _expert_call(rows2d, tile0, n_tile, chunk0, n_chunk, tile_row, chunk_row, w1, w3, w2a, w2b):
    ne, d, hdim = w1.shape
    per_expert = lambda w: pl.BlockSpec((1,) + w.shape[1:], lambda e, *_: (e, 0, 0))
    stream_scratch = lambda tr: [pltpu.VMEM((2, tr * ROW_TILES, LANES), BF16),
                                 pltpu.VMEM((2, tr * ROW_TILES, LANES), BF16),
                                 pltpu.SemaphoreType.DMA((2,)), pltpu.SemaphoreType.DMA((2,))]
    grid_spec = pltpu.PrefetchScalarGridSpec(
        num_scalar_prefetch=6,
        grid=(ne,),
        in_specs=[pl.BlockSpec(memory_space=pl.ANY),
                  per_expert(w1), per_expert(w3), per_expert(w2a), per_expert(w2b)],
        out_specs=pl.BlockSpec(memory_space=pl.ANY),
        scratch_shapes=stream_scratch(EXPERT_TILE) + stream_scratch(EXPERT_CHUNK),
    )
    return pl.pallas_call(
        _expert_kernel,
        grid_spec=grid_spec,
        out_shape=jax.ShapeDtypeStruct(rows2d.shape, BF16),
        compiler_params=_params(("arbitrary",)),
        name="routed_experts",
    )(tile0, n_tile, chunk0, n_chunk, tile_row, chunk_row, rows2d, w1, w3, w2a, w2b)


def _combine_kernel(cnt_ref, start_ref, off_ref, pos_ref, wts_ref, rows_hbm, s_ref, g_ref, b_ref,
                    o32_ref, o16_ref, buf, acc, sem):
    w = pl.program_id(0)
    last = pl.num_programs(0) - 1
    slot = w % 2
    start = lambda c: c.start()
    wait = lambda c: c.wait()
    copies = functools.partial(_segment_copies, cnt_ref, start_ref, off_ref,
                               remote=rows_hbm, to_remote=False)

    @pl.when(w == 0)
    def _():
        copies(win=w, local=buf.at[slot], sem=sem.at[slot], act=start)

    @pl.when(w < last)
    def _():
        copies(win=w + 1, local=buf.at[1 - slot], sem=sem.at[1 - slot], act=start)

    _wait_window(buf.at[slot], rows_hbm, sem.at[slot], False)

    def token(t, carry):
        total = wts_ref[0, 0, t] * buf[slot, pos_ref[0, 0, t]].astype(F32)
        for k in range(1, TOP_K):
            total = total + wts_ref[0, 0, k * WINDOW + t] * buf[slot, pos_ref[0, 0, k * WINDOW + t]].astype(F32)
        acc[t] = total
        return carry
    lax.fori_loop(0, WINDOW, token, 0, unroll=16)

    by_chunk = jnp.swapaxes(acc[...], 0, 1)
    routed = jnp.concatenate([by_chunk[c] for c in range(ROW_TILES)], axis=1)
    out = _layer_norm_rows(s_ref[...] + routed, g_ref[...], b_ref[...])
    o32_ref[...] = out
    o16_ref[...] = out.astype(BF16)


def _combine_cast_kernel(cnt_ref, start_ref, off_ref, pos_ref, wts_ref, rows_hbm, s_ref, g_ref, b_ref, ew_ref,
                         o32_ref, o16_ref, ew16_ref, *scratch):
    ew16_ref[...] = ew_ref[0].astype(BF16)
    _combine_kernel(cnt_ref, start_ref, off_ref, pos_ref, wts_ref, rows_hbm, s_ref, g_ref, b_ref,
                    o32_ref, o16_ref, *scratch)


def _combine_call(rows_tiles, pos, wts, cnt, start, off, s, g, b, stream):
    t, d = s.shape
    nw = t // WINDOW
    smem_block = pl.BlockSpec((1, 1, PAIRS), lambda w, *_: (w, 0, 0), memory_space=pltpu.SMEM)
    row = pl.BlockSpec((WINDOW, d), lambda w, *_: (w, 0))
    vec = pl.BlockSpec((1, d), lambda w, *_: (0, 0))
    in_specs = [smem_block, smem_block, pl.BlockSpec(memory_space=pl.ANY), row, vec, vec]
    out_specs = [row, row]
    out_shape = [jax.ShapeDtypeStruct((t, d), F32), jax.ShapeDtypeStruct((t, d), BF16)]
    extra = ()
    if stream is not None:
        w4, ew_in, ew_out, ew_shape = stream
        in_specs, out_specs, out_shape, extra = in_specs + [ew_in], out_specs + [ew_out], out_shape + [ew_shape], (w4,)
    grid_spec = pltpu.PrefetchScalarGridSpec(
        num_scalar_prefetch=3,
        grid=(nw,),
        in_specs=in_specs,
        out_specs=out_specs,
        scratch_shapes=[pltpu.VMEM((2, PAIRS, ROW_TILES, LANES), BF16),
                        pltpu.VMEM((WINDOW, ROW_TILES, LANES), F32),
                        pltpu.SemaphoreType.DMA((2,))],
    )
    out = pl.pallas_call(
        _combine_kernel if stream is None else _combine_cast_kernel,
        grid_spec=grid_spec,
        out_shape=out_shape,
        compiler_params=_params(("arbitrary",)),
        name="combine_ln",
    )(cnt, start, off, pos, wts, rows_tiles, s, g.reshape(1, d), b.reshape(1, d), *extra)
    return (*out, None) if stream is None else out


def _block_rows(first_block, n_blocks, first_row, rows_per_block, capacity):
    block = jnp.arange(capacity, dtype=I32)
    ends = first_block + n_blocks
    owner = jnp.minimum(jnp.sum((ends[None, :] <= block[:, None]).astype(I32), axis=1), N_EXPERTS - 1)
    return (first_row[owner] + (block - first_block[owner]) * rows_per_block).astype(I32)


def _row_layout(cnt, n_tokens):
    cnt2 = cnt.reshape(-1, N_EXPERTS)
    total = jnp.sum(cnt2, axis=0)
    padded = (total + EXPERT_CHUNK - 1) // EXPERT_CHUNK * EXPERT_CHUNK
    base = jnp.cumsum(padded) - padded
    off = base[None, :] + jnp.cumsum(cnt2, axis=0) - cnt2
    n_tile = padded // EXPERT_TILE
    n_chunk = (padded - n_tile * EXPERT_TILE) // EXPERT_CHUNK
    tile0 = jnp.cumsum(n_tile) - n_tile
    chunk0 = jnp.cumsum(n_chunk) - n_chunk
    tile_row = _block_rows(tile0, n_tile, base, EXPERT_TILE, n_tokens * TOP_K // EXPERT_TILE + N_EXPERTS)
    chunk_row = _block_rows(chunk0, n_chunk, base + n_tile * EXPERT_TILE, EXPERT_CHUNK,
                            N_EXPERTS * (EXPERT_TILE // EXPERT_CHUNK - 1))
    as_i32 = lambda *arrays: tuple(a.astype(I32) for a in arrays)
    return (as_i32(off.reshape(-1), base + total, padded - total),
            as_i32(tile0, n_tile, chunk0, n_chunk, tile_row, chunk_row))


def _moe_block(x32, x16, s, w_router, router_bias, w1_16, w3_16, w2a_16, w2b_16, g, b, next_layer, w_in, w_gate):
    t, d = x32.shape
    nw = t // WINDOW
    n_rows = t * TOP_K + N_EXPERTS * EXPERT_CHUNK
    wts, pos, cnt, start = _router_call(x32, w_router, router_bias)
    (off, pad_start, pad_len), blocks = _row_layout(cnt, t)
    cnt = cnt.reshape(-1)
    start = start.reshape(-1)
    pos = pos.reshape(nw, 1, PAIRS)
    wts = wts.reshape(nw, 1, PAIRS)
    rows, w_in16 = _dispatch_call(x16.reshape(t, ROW_TILES, LANES), pos, cnt, start, off, pad_start, pad_len,
                                  n_rows, _next_layer_stream(w_in, next_layer, nw))
    out_rows = _expert_call(rows.reshape(n_rows * ROW_TILES, LANES), *blocks, w1_16, w3_16, w2a_16, w2b_16)
    y32, y16, w_gate16 = _combine_call(out_rows.reshape(n_rows, ROW_TILES, LANES), pos, wts, cnt, start, off,
                                       s, g, b, _next_layer_stream(w_gate, next_layer, nw))
    return y32, y16, w_in16, w_gate16


def kernel(x, in_ln_g, in_ln_b, w_in, w_spatial, b_spatial, w_conv, w_pool, pool_scale,
           w_gate, b_gate, w_proj_a, w_proj_b, w_proj_c, w_out, ln1_g, ln1_b,
           w_router, router_bias, w_expert_gate, w_expert_up, w_expert_down,
           w_shared_gate, w_shared_up, w_shared_down, ln2_g, ln2_b):
    bsz, seq, d = x.shape
    x32, x16 = _ln_call(x.reshape(bsz * seq, d), in_ln_g, in_ln_b)
    w_in16, w_gate16 = w_in[0].astype(BF16), w_gate[0].astype(BF16)
    for l in range(DEPTH):
        ya, yb, yc, w1_16, w2a_16 = _mix_tokens_call(
            x16, w_in16, w_spatial[l], b_spatial[l], w_conv[l], w_pool[l].astype(BF16),
            pool_scale[l], l, w_expert_gate, w_expert_down)
        merged, w3_16, w2b_16 = _merge_call(
            x16, ya, yb, yc, w_gate16, b_gate[l], w_proj_a[l].astype(BF16),
            w_proj_b[l].astype(BF16), w_proj_c[l].astype(BF16), l, w_expert_up, w_expert_down)
        x32, x16, s = _out_proj_call(merged, x32, w_out[l].astype(BF16), ln1_g[l], ln1_b[l],
                                     w_shared_gate[l].astype(BF16), w_shared_up[l].astype(BF16),
                                     w_shared_down[l].astype(BF16))
        x32, x16, w_in16, w_gate16 = _moe_block(
            x32, x16, s, w_router[l], router_bias[l], w1_16, w3_16, w2a_16, w2b_16, ln2_g[l], ln2_b[l],
            l + 1, w_in, w_gate)
        if w_in16 is not None:
            w_in16, w_gate16 = w_in16.reshape(w_in.shape[1:]), w_gate16.reshape(w_gate.shape[1:])
    return x32.reshape(bsz, seq, d)
```
